```python
import jax, jax.numpy as jnp
from jax import lax
import numpy as np

D_MODEL = 1024
BATCH = 8
SEQ = 4096
DEPTH = 4

CTX_LEN = 256
GRID_W = 64
CONV_W = 1024
CONV_K = 31
LRU_W = 1024
LRU_HEADS = 16
LRU_HD = LRU_W // LRU_HEADS
LRU_CONV_K = 4
LRU_PAD = ((LRU_CONV_K - 1) // 2, LRU_CONV_K // 2)
LRU_C = 8.0
FFN_W = 2560
FFN_K = 3
N_MOD = 6
EPS = 1e-6
IN_W = 2 * CONV_W + 2 * LRU_W + 2 * D_MODEL
SPLIT_IN = (2 * CONV_W, 2 * CONV_W + LRU_W, 2 * CONV_W + 2 * LRU_W)

kernel_name = 'hybrid_conformer_rglru_prefix_dit'


def _rmsnorm(x, g):
    xf = x.astype(jnp.float32)
    y = xf * lax.rsqrt(jnp.mean(xf * xf, axis=-1, keepdims=True) + EPS)
    return (y * g.astype(jnp.float32)).astype(x.dtype)


def _layernorm(x, g, b):
    xf = x.astype(jnp.float32)
    mu = jnp.mean(xf, axis=-1, keepdims=True)
    var = jnp.mean(jnp.square(xf - mu), axis=-1, keepdims=True)
    y = (xf - mu) * lax.rsqrt(var + EPS)
    return (y * g.astype(jnp.float32) + b.astype(jnp.float32)).astype(x.dtype)


def _dwconv1d(x, w, pad):
    return lax.conv_general_dilated(
        x, w.astype(x.dtype)[:, None, :], window_strides=(1,), padding=[pad],
        dimension_numbers=('NWC', 'WIO', 'NWC'), feature_group_count=x.shape[-1])


def _dwconv_grid(x, w, rows):
    b, t, ch = x.shape
    xg = x.reshape(b, rows, GRID_W, ch)
    y = lax.conv_general_dilated(
        xg, w.astype(x.dtype)[:, :, None, :], window_strides=(1, 1),
        padding=[(FFN_K // 2, FFN_K // 2), (FFN_K // 2, FFN_K // 2)],
        dimension_numbers=('NHWC', 'HWIO', 'NHWC'), feature_group_count=ch)
    return y.reshape(b, t, ch)


def _modulate(h, shift, scale):
    return h * (1 + scale) + shift


def _rglru_coeffs(xr, w_r, b_r, w_i, b_i, lam):
    b, t, _ = xr.shape
    xf = xr.astype(jnp.float32)
    xh = xf.reshape(b, t, LRU_HEADS, LRU_HD)
    r = jax.nn.sigmoid(jnp.einsum('bthi,hij->bthj', xh, w_r.astype(jnp.float32)).reshape(b, t, LRU_W)
                       + b_r.astype(jnp.float32))
    i = jax.nn.sigmoid(jnp.einsum('bthi,hij->bthj', xh, w_i.astype(jnp.float32)).reshape(b, t, LRU_W)
                       + b_i.astype(jnp.float32))
    log_a = -LRU_C * r * jax.nn.softplus(-lam.astype(jnp.float32))
    a = jnp.exp(log_a)
    u = jnp.sqrt(-jnp.expm1(2.0 * log_a)) * (i * xf)
    return a, u


def _linear_scan(a, u, h0):
    def combine(left, right):
        a_l, b_l = left
        a_r, b_r = right
        return a_l * a_r, a_r * b_l + b_r
    a_cum, h = lax.associative_scan(combine, (a, u), axis=1)
    if h0 is None:
        return h
    return h + a_cum * h0[:, None, :]


def _scan_direction(a_c, u_c, a_l, u_l, reverse):
    if reverse:
        a_c, u_c, a_l, u_l = a_c[:, ::-1], u_c[:, ::-1], a_l[:, ::-1], u_l[:, ::-1]
    h_c = _linear_scan(a_c, u_c, None)
    h_l = _linear_scan(a_l, u_l, h_c[:, -1])
    if reverse:
        h_c, h_l = h_c[:, ::-1], h_l[:, ::-1]
    return h_c, h_l


def _conformer_conv(glu, dw, ln_g, ln_b, w_proj):
    v, g = jnp.split(glu, 2, axis=-1)
    u = _dwconv1d(v * jax.nn.sigmoid(g), dw, (CONV_K // 2, CONV_K // 2))
    return jax.nn.silu(_layernorm(u, ln_g, ln_b)) @ w_proj


def _token_mixer(hl, hc, w_in, dw_conv, ln_conv_g, ln_conv_b, w_proj_conv, lru_conv_w, lru_conv_b,
                 w_rgate, b_rgate, w_igate, b_igate, lru_lambda, w_proj_lru, w_out, need_ctx):
    glu_l, xr_l, gt_l, mg_l = jnp.split(hl @ w_in, SPLIT_IN, axis=-1)
    if need_ctx:
        glu_c, xr_c, gt_c, mg_c = jnp.split(hc @ w_in, SPLIT_IN, axis=-1)
    else:
        xr_c = hc @ w_in[:, SPLIT_IN[0]:SPLIT_IN[1]]
    xr_l = _dwconv1d(xr_l, lru_conv_w, LRU_PAD) + lru_conv_b
    xr_c = _dwconv1d(xr_c, lru_conv_w, LRU_PAD) + lru_conv_b
    rec_l = None
    rec_c = None
    for d in range(2):
        a_c, u_c = _rglru_coeffs(xr_c, w_rgate[d], b_rgate[d], w_igate[d], b_igate[d], lru_lambda[d])
        a_l, u_l = _rglru_coeffs(xr_l, w_rgate[d], b_rgate[d], w_igate[d], b_igate[d], lru_lambda[d])
        h_c, h_l = _scan_direction(a_c, u_c, a_l, u_l, reverse=(d == 1))
        rec_l = h_l if rec_l is None else rec_l + h_l
        rec_c = h_c if rec_c is None else rec_c + h_c

    def merge(glu, rec, gt, mg):
        y_a = _conformer_conv(glu, dw_conv, ln_conv_g, ln_conv_b, w_proj_conv)
        y_b = (rec.astype(gt.dtype) * jax.nn.gelu(gt)) @ w_proj_lru
        g_a, g_b = jnp.split(mg, 2, axis=-1)
        return (jax.nn.sigmoid(g_a) * y_a + jax.nn.sigmoid(g_b) * y_b) @ w_out

    y_l = merge(glu_l, rec_l, gt_l, mg_l)
    y_c = merge(glu_c, rec_c, gt_c, mg_c) if need_ctx else None
    return y_l, y_c


def _conv_ffn(h, w_up, dw, w_down, rows):
    z = h @ w_up
    if rows is None:
        z = _dwconv1d(z, dw[FFN_K // 2], (FFN_K // 2, FFN_K // 2))
    else:
        z = _dwconv_grid(z, dw, rows)
    v, g = jnp.split(z, 2, axis=-1)
    return (jax.nn.silu(g) * v) @ w_down


def setup_inputs(seed: int = 0) -> dict:
    key = jax.random.key(seed)
    ks = jax.random.split(key, 32)

    def nrm(k, shape, scale):
        return jax.random.normal(k, shape, jnp.float32) * scale

    def gain(k, shape):
        return 1.0 + 0.05 * jax.random.normal(k, shape, jnp.float32)

    u = jax.random.uniform(ks[21], (DEPTH, 2, LRU_W), jnp.float32, minval=0.9, maxval=0.999)
    a = u ** (1.0 / LRU_C)
    lam = jnp.log(a) - jnp.log1p(-a)
    return {
        'x': nrm(ks[0], (BATCH, SEQ, D_MODEL), 1.0),
        'c': nrm(ks[1], (BATCH, D_MODEL), 1.0),
        'ctx': nrm(ks[2], (BATCH, CTX_LEN, D_MODEL), 1.0),
        'c_ctx': nrm(ks[3], (D_MODEL,), 1.0),
        'w_ada': nrm(ks[4], (DEPTH, D_MODEL, N_MOD * D_MODEL), 0.5 * D_MODEL ** -0.5),
        'b_ada': nrm(ks[5], (DEPTH, N_MOD * D_MODEL), 0.02),
        'g_pre_mix': gain(ks[6], (DEPTH, D_MODEL)),
        'g_post_mix': gain(ks[7], (DEPTH, D_MODEL)),
        'g_pre_ffn': gain(ks[8], (DEPTH, D_MODEL)),
        'g_post_ffn': gain(ks[9], (DEPTH, D_MODEL)),
        'w_in': nrm(ks[10], (DEPTH, D_MODEL, IN_W), D_MODEL ** -0.5),
        'dw_conv': nrm(ks[11], (DEPTH, CONV_K, CONV_W), CONV_K ** -0.5),
        'ln_conv_g': gain(ks[12], (DEPTH, CONV_W)),
        'ln_conv_b': nrm(ks[13], (DEPTH, CONV_W), 0.02),
        'w_proj_conv': nrm(ks[14], (DEPTH, CONV_W, D_MODEL), CONV_W ** -0.5),
        'lru_conv_w': nrm(ks[15], (DEPTH, LRU_CONV_K, LRU_W), LRU_CONV_K ** -0.5),
        'lru_conv_b': nrm(ks[16], (DEPTH, LRU_W), 0.02),
        'w_rgate': nrm(ks[17], (DEPTH, 2, LRU_HEADS, LRU_HD, LRU_HD), LRU_HD ** -0.5),
        'b_rgate': nrm(ks[18], (DEPTH, 2, LRU_W), 0.02),
        'w_igate': nrm(ks[19], (DEPTH, 2, LRU_HEADS, LRU_HD, LRU_HD), LRU_HD ** -0.5),
        'b_igate': nrm(ks[20], (DEPTH, 2, LRU_W), 0.02),
        'lru_lambda': lam,
        'w_proj_lru': nrm(ks[22], (DEPTH, LRU_W, D_MODEL), LRU_W ** -0.5),
        'w_out': nrm(ks[23], (DEPTH, D_MODEL, D_MODEL), D_MODEL ** -0.5),
        'w_up': nrm(ks[24], (DEPTH, D_MODEL, 2 * FFN_W), D_MODEL ** -0.5),
        'dw_ffn': nrm(ks[25], (DEPTH, FFN_K, FFN_K, 2 * FFN_W), 1.0 / FFN_K),
        'w_down': nrm(ks[26], (DEPTH, FFN_W, D_MODEL), FFN_W ** -0.5),
    }


def reference(x, c, ctx, c_ctx, w_ada, b_ada, g_pre_mix, g_post_mix, g_pre_ffn, g_post_ffn,
              w_in, dw_conv, ln_conv_g, ln_conv_b, w_proj_conv, lru_conv_w, lru_conv_b,
              w_rgate, b_rgate, w_igate, b_igate, lru_lambda, w_proj_lru, w_out,
              w_up, dw_ffn, w_down):
    rows = x.shape[1] // GRID_W
    s_lat = jax.nn.silu(c)
    s_ctx = jax.nn.silu(c_ctx)
    for l in range(DEPTH):
        need_ctx = l < DEPTH - 1
        sh1, sc1, gt1, sh2, sc2, gt2 = jnp.split((s_lat @ w_ada[l] + b_ada[l])[:, None, :], N_MOD, axis=-1)
        csh1, csc1, cgt1, csh2, csc2, cgt2 = jnp.split((s_ctx @ w_ada[l] + b_ada[l])[None, None, :], N_MOD, axis=-1)

        hl = _modulate(_rmsnorm(x, g_pre_mix[l]), sh1, sc1)
        hc = _modulate(_rmsnorm(ctx, g_pre_mix[l]), csh1, csc1)
        y_l, y_c = _token_mixer(hl, hc, w_in[l], dw_conv[l], ln_conv_g[l], ln_conv_b[l], w_proj_conv[l],
                                lru_conv_w[l], lru_conv_b[l], w_rgate[l], b_rgate[l], w_igate[l], b_igate[l],
                                lru_lambda[l], w_proj_lru[l], w_out[l], need_ctx)
        x = x + gt1 * _rmsnorm(y_l, g_post_mix[l])

        h = _modulate(_rmsnorm(x, g_pre_ffn[l]), sh2, sc2)
        x = x + gt2 * _rmsnorm(_conv_ffn(h, w_up[l], dw_ffn[l], w_down[l], rows), g_post_ffn[l])

        if need_ctx:
            ctx = ctx + cgt1 * _rmsnorm(y_c, g_post_mix[l])
            hc2 = _modulate(_rmsnorm(ctx, g_pre_ffn[l]), csh2, csc2)
            ctx = ctx + cgt2 * _rmsnorm(_conv_ffn(hc2, w_up[l], dw_ffn[l], w_down[l], None), g_post_ffn[l])
    return x
```

```python
import functools

import jax
import jax.numpy as jnp
from jax import lax
from jax.experimental import pallas as pl
from jax.experimental.pallas import tpu as pltpu

F32 = jnp.float32
BF16 = jnp.bfloat16

NB = 8
EPS = 1e-6
GRID_W = 64
LRU_C = 8.0
GATE_BLK = 256
V7X_VMEM_LIMIT = 56 * 1024 * 1024
CONV_HALO_ROWS = 128
LRU_HALO_ROWS = 16
FFN_HALO_ROWS = 16
TM_CHUNK = 8


def _const_spec(shape):
    nd = len(shape)
    return pl.BlockSpec(shape, lambda *_: (0,) * nd, pipeline_mode=pl.Buffered(1))


def _params(sem):
    return pltpu.CompilerParams(dimension_semantics=sem, vmem_limit_bytes=V7X_VMEM_LIMIT)


def _rms(x, g):
    ms = jnp.mean(x * x, axis=-1, keepdims=True)
    return x * lax.rsqrt(ms + EPS) * g


def _per_batch(x, mul, add):
    r, d = x.shape
    y = x.reshape(r // NB, NB, d) * mul[None]
    if add is not None:
        y = y + add[None]
    return y.reshape(r, d)


def _softplus(z):
    return jnp.maximum(z, 0.0) + jnp.log1p(jnp.exp(-jnp.abs(z)))


def _ada_kernel(c_ref, w_ref, b_ref, o_ref):
    s = jax.nn.silu(c_ref[...])
    o_ref[0] = jnp.dot(s, w_ref[0], preferred_element_type=F32) + b_ref[0]


def _ada_call(cc, w_ada, b_ada):
    depth, d, n = w_ada.shape
    tn = 1024 if n % 1024 == 0 else n
    return pl.pallas_call(
        _ada_kernel,
        grid=(depth, n // tn),
        in_specs=[pl.BlockSpec(cc.shape, lambda l, j: (0, 0)),
                  pl.BlockSpec((1, d, tn), lambda l, j: (l, 0, j)),
                  pl.BlockSpec((1, 1, tn), lambda l, j: (l, 0, j))],
        out_specs=pl.BlockSpec((1, cc.shape[0], tn), lambda l, j: (l, 0, j)),
        out_shape=jax.ShapeDtypeStruct((depth, cc.shape[0], n), F32),
        compiler_params=_params(("arbitrary", "arbitrary")),
        name="ada_mod",
    )(cc, w_ada, b_ada.reshape(depth, 1, n))


def _mixer_front_kernel(x_ref, mod_ref, g_ref, w_ref, act_ref, xr_ref, ggt_ref, smg_ref, *, cw, lw):
    r, d = x_ref.shape
    h = _rms(x_ref[...], g_ref[...])
    h = _per_batch(h, 1.0 + mod_ref[:, d:2 * d], mod_ref[:, 0:d])
    hb = h.astype(BF16)

    def mm(lo, hi):
        return jnp.dot(hb, w_ref[:, lo:hi], preferred_element_type=F32)

    o = 0
    v = mm(o, o + cw)
    g = mm(o + cw, o + 2 * cw)
    act_ref[...] = (v * jax.nn.sigmoid(g)).astype(act_ref.dtype)
    o += 2 * cw
    xr_ref[...] = mm(o, o + lw)
    o += lw
    ggt_ref[...] = jax.nn.gelu(mm(o, o + lw)).astype(ggt_ref.dtype)
    o += lw
    smg_ref[:, 0:d] = jax.nn.sigmoid(mm(o, o + d)).astype(smg_ref.dtype)
    smg_ref[:, d:2 * d] = jax.nn.sigmoid(mm(o + d, o + 2 * d)).astype(smg_ref.dtype)


def _mixer_front(x, mod, g_pre, w_in, cw, lw, tr):
    rows, d = x.shape
    kern = functools.partial(_mixer_front_kernel, cw=cw, lw=lw)
    row = lambda i: (i, 0)
    return pl.pallas_call(
        kern,
        grid=(rows // tr,),
        in_specs=[pl.BlockSpec((tr, d), row), _const_spec(mod.shape), _const_spec(g_pre.shape),
                  _const_spec(w_in.shape)],
        out_specs=[pl.BlockSpec((tr, cw), row), pl.BlockSpec((tr, lw), row),
                   pl.BlockSpec((tr, lw), row), pl.BlockSpec((tr, 2 * d), row)],
        out_shape=[jax.ShapeDtypeStruct((rows, cw), BF16), jax.ShapeDtypeStruct((rows, lw), F32),
                   jax.ShapeDtypeStruct((rows, lw), BF16), jax.ShapeDtypeStruct((rows, 2 * d), BF16)],
        compiler_params=_params(("arbitrary",)),
        name="mixer_front",
    )(x, mod, g_pre, w_in)


def _lru_coeffs(xc, wg_ref, br_ref, bi_ref, lam_ref, a_s, u_s):
    xb = xc.astype(BF16)
    nblk = wg_ref.shape[0]
    for blk in range(nblk):
        lo, hi = blk * GATE_BLK, (blk + 1) * GATE_BLK
        g = jnp.dot(xb[:, lo:hi], wg_ref[blk], preferred_element_type=F32)
        rg = jax.nn.sigmoid(g[:, :GATE_BLK] + br_ref[:, lo:hi])
        ig = jax.nn.sigmoid(g[:, GATE_BLK:] + bi_ref[:, lo:hi])
        log_a = (-LRU_C * _softplus(-lam_ref[:, lo:hi])) * rg
        a_s[:, lo:hi] = jnp.exp(log_a)
        th = jnp.tanh(log_a)
        u_s[:, lo:hi] = jnp.sqrt(-2.0 * th / (1.0 - th)) * (ig * xc[:, lo:hi])


def _scan_bwd_kernel(xp_ref, xm_ref, xn_ref, cw_ref, cb_ref, wg_ref, br_ref, bi_ref, lam_ref, h0_ref,
                     xc_ref, hb_ref, st_ref, buf, a_s, u_s, h_s):
    i = pl.program_id(0)
    n = pl.num_programs(0)
    j = n - 1 - i
    r = xm_ref.shape[0]
    k_taps = cw_ref.shape[0]

    @pl.when(i == 0)
    def _():
        h_s[...] = h0_ref[...]

    buf[0:NB, :] = jnp.where(j > 0, xp_ref[LRU_HALO_ROWS - NB:, :], 0.0)
    buf[NB:NB + r, :] = xm_ref[...]
    buf[NB + r:, :] = jnp.where(j < n - 1, xn_ref[...], 0.0)
    xc = cb_ref[...] + cw_ref[0:1, :] * buf[0:r, :]
    for k in range(1, k_taps):
        xc = xc + cw_ref[k:k + 1, :] * buf[k * NB:k * NB + r, :]
    xc_ref[...] = xc
    _lru_coeffs(xc, wg_ref, br_ref, bi_ref, lam_ref, a_s, u_s)

    nt = r // NB

    def step(s, h):
        r0 = pl.multiple_of((nt - 1 - s) * NB, NB)
        h = a_s[pl.ds(r0, NB), :] * h + u_s[pl.ds(r0, NB), :]
        hb_ref[pl.ds(r0, NB), :] = h
        return h

    h = lax.fori_loop(0, nt, step, h_s[...], unroll=8)
    h_s[...] = h
    st_ref[...] = h


def _scan_bwd(xr, conv_w, conv_b, wg, br, bi, lam, h0, tr):
    rows, w = xr.shape
    n = rows // tr
    hb = tr // LRU_HALO_ROWS
    last = rows // LRU_HALO_ROWS - 1
    return pl.pallas_call(
        _scan_bwd_kernel,
        grid=(n,),
        in_specs=[pl.BlockSpec((LRU_HALO_ROWS, w), lambda i: (jnp.maximum((n - 1 - i) * hb - 1, 0), 0)),
                  pl.BlockSpec((tr, w), lambda i: (n - 1 - i, 0)),
                  pl.BlockSpec((LRU_HALO_ROWS, w), lambda i: (jnp.minimum((n - i) * hb, last), 0)),
                  _const_spec(conv_w.shape), _const_spec(conv_b.shape), _const_spec(wg.shape),
                  _const_spec(br.shape), _const_spec(bi.shape), _const_spec(lam.shape),
                  _const_spec(h0.shape)],
        out_specs=[pl.BlockSpec((tr, w), lambda i: (n - 1 - i, 0)),
                   pl.BlockSpec((tr, w), lambda i: (n - 1 - i, 0)),
                   pl.BlockSpec((NB, w), lambda i: (0, 0))],
        out_shape=[jax.ShapeDtypeStruct((rows, w), F32), jax.ShapeDtypeStruct((rows, w), F32),
                   jax.ShapeDtypeStruct((NB, w), F32)],
        scratch_shapes=[pltpu.VMEM((tr + NB + LRU_HALO_ROWS, w), F32), pltpu.VMEM((tr, w), F32),
                        pltpu.VMEM((tr, w), F32), pltpu.VMEM((NB, w), F32)],
        compiler_params=_params(("arbitrary",)),
        name="lru_scan_bwd",
    )(xr, xr, xr, conv_w, conv_b, wg, br, bi, lam, h0)


def _dwconv_time(src, dst, w_ref, rows, k_taps):
    lanes = dst.shape[1]
    span = TM_CHUNK + k_taps - 1
    for cb in range(lanes // 128):
        lo, hi = cb * 128, (cb + 1) * 128
        wk = [jnp.broadcast_to(w_ref[k:k + 1, lo:hi], (NB, 128)) for k in range(k_taps)]

        def chunk(c, carry, lo=lo, hi=hi, wk=wk):
            base = pl.multiple_of(c * (TM_CHUNK * NB), TM_CHUNK * NB)
            ins = [src[pl.ds(base + q * NB, NB), lo:hi] for q in range(span)]
            outs = []
            for m in range(TM_CHUNK):
                acc = wk[0] * ins[m]
                for k in range(1, k_taps):
                    acc = acc + wk[k] * ins[m + k]
                outs.append(acc)
            dst[pl.ds(base, TM_CHUNK * NB), lo:hi] = jnp.concatenate(outs, axis=0)
            return carry

        lax.fori_loop(0, rows // (TM_CHUNK * NB), chunk, 0)


def _mixer_back_kernel(ap_ref, am_ref, an_ref, xc_ref, hb_ref, ggt_ref, smg_ref, x_ref, mod_ref,
                       dw_ref, lng_ref, lnb_ref, wpc_ref, wg_ref, br_ref, bi_ref, lam_ref, wpl_ref,
                       wo_ref, gpost_ref, h0_ref, x1_ref, st_ref, cbuf, cout, a_s, u_s, h_s):
    i = pl.program_id(0)
    n = pl.num_programs(0)
    r, d = x_ref.shape
    k_taps = dw_ref.shape[0]
    halo = (k_taps // 2) * NB

    @pl.when(i == 0)
    def _():
        h_s[...] = h0_ref[...]

    prev = ap_ref[...].astype(F32)
    nxt = an_ref[...].astype(F32)
    cbuf[0:halo, :] = jnp.where(i > 0, prev[CONV_HALO_ROWS - halo:, :], 0.0)
    cbuf[halo:halo + r, :] = am_ref[...].astype(F32)
    cbuf[halo + r:, :] = jnp.where(i < n - 1, nxt[0:halo, :], 0.0)
    _dwconv_time(cbuf, cout, dw_ref, r, k_taps)
    u = cout[...]
    mu = jnp.mean(u, axis=-1, keepdims=True)
    uc = u - mu
    var = jnp.mean(uc * uc, axis=-1, keepdims=True)
    ln = uc * lax.rsqrt(var + EPS) * lng_ref[...] + lnb_ref[...]
    y_a = jnp.dot(jax.nn.silu(ln).astype(BF16), wpc_ref[...], preferred_element_type=F32)

    _lru_coeffs(xc_ref[...], wg_ref, br_ref, bi_ref, lam_ref, a_s, u_s)
    nt = r // NB

    def step(s, h):
        r0 = pl.multiple_of(s * NB, NB)
        h = a_s[pl.ds(r0, NB), :] * h + u_s[pl.ds(r0, NB), :]
        cout[pl.ds(r0, NB), :] = h + hb_ref[pl.ds(r0, NB), :]
        return h

    h = lax.fori_loop(0, nt, step, h_s[...], unroll=8)
    h_s[...] = h
    st_ref[...] = h
    rec = cout[...]
    y_b = jnp.dot((rec * ggt_ref[...].astype(F32)).astype(BF16), wpl_ref[...], preferred_element_type=F32)

    m = smg_ref[:, 0:d].astype(F32) * y_a + smg_ref[:, d:2 * d].astype(F32) * y_b
    y = jnp.dot(m.astype(BF16), wo_ref[...], preferred_element_type=F32)
    x1_ref[...] = x_ref[...] + _per_batch(_rms(y, gpost_ref[...]), mod_ref[:, 2 * d:3 * d], None)


def _mixer_back(act, xc, hb, ggt, smg, x, mod, dw, lng, lnb, wpc, wg, br, bi, lam, wpl, wo, gpost, h0, tr):
    rows, d = x.shape
    cw = act.shape[1]
    w = xc.shape[1]
    assert cw == w, "conv and LRU branch widths share one scratch buffer"
    n = rows // tr
    hb_blocks = tr // CONV_HALO_ROWS
    last = rows // CONV_HALO_ROWS - 1
    k_taps = dw.shape[0]
    halo = (k_taps // 2) * NB
    assert halo <= CONV_HALO_ROWS
    row = lambda i: (i, 0)
    consts = [mod, dw, lng, lnb, wpc, wg, br, bi, lam, wpl, wo, gpost, h0]
    return pl.pallas_call(
        _mixer_back_kernel,
        grid=(n,),
        in_specs=[pl.BlockSpec((CONV_HALO_ROWS, cw), lambda i: (jnp.maximum(i * hb_blocks - 1, 0), 0)),
                  pl.BlockSpec((tr, cw), row),
                  pl.BlockSpec((CONV_HALO_ROWS, cw), lambda i: (jnp.minimum((i + 1) * hb_blocks, last), 0)),
                  pl.BlockSpec((tr, w), row), pl.BlockSpec((tr, w), row), pl.BlockSpec((tr, w), row),
                  pl.BlockSpec((tr, 2 * d), row), pl.BlockSpec((tr, d), row)]
                 + [_const_spec(a.shape) for a in consts],
        out_specs=[pl.BlockSpec((tr, d), row), pl.BlockSpec((NB, w), lambda i: (0, 0))],
        out_shape=[jax.ShapeDtypeStruct((rows, d), F32), jax.ShapeDtypeStruct((NB, w), F32)],
        scratch_shapes=[pltpu.VMEM((tr + 2 * halo, cw), F32), pltpu.VMEM((tr, cw), F32),
                        pltpu.VMEM((tr, w), F32), pltpu.VMEM((tr, w), F32), pltpu.VMEM((NB, w), F32)],
        compiler_params=_params(("arbitrary",)),
        name="mixer_back",
    )(act, act, act, xc, hb, ggt, smg, x, *consts)


def _ffn_up_kernel(x_ref, mod_ref, g_ref, w_ref, z_ref, *, tn):
    r, d = x_ref.shape
    h = _rms(x_ref[...], g_ref[...])
    h = _per_batch(h, 1.0 + mod_ref[:, 4 * d:5 * d], mod_ref[:, 3 * d:4 * d])
    hb = h.astype(BF16)
    for c in range(w_ref.shape[1] // tn):
        z_ref[:, c * tn:(c + 1) * tn] = jnp.dot(
            hb, w_ref[:, c * tn:(c + 1) * tn], preferred_element_type=F32).astype(z_ref.dtype)


def _ffn_up(x, mod, g_pre, w_up, tr):
    rows, d = x.shape
    n2 = w_up.shape[1]
    tn = 1280 if n2 % 1280 == 0 else n2
    row = lambda i: (i, 0)
    return pl.pallas_call(
        functools.partial(_ffn_up_kernel, tn=tn),
        grid=(rows // tr,),
        in_specs=[pl.BlockSpec((tr, d), row), _const_spec(mod.shape), _const_spec(g_pre.shape),
                  _const_spec(w_up.shape)],
        out_specs=pl.BlockSpec((tr, n2), row),
        out_shape=jax.ShapeDtypeStruct((rows, n2), BF16),
        compiler_params=_params(("arbitrary",)),
        name="ffn_up",
    )(x, mod, g_pre, w_up)


def _ffn_back_kernel(vp_ref, vm_ref, vn_ref, gp_ref, gm_ref, gn_ref, dwv_ref, dwg_ref, wd_ref, x_ref,
                     mod_ref, gpost_ref, o_ref, zv, zg, hh, acc, *, vert, period, vhalo):
    j = pl.program_id(0)
    nj = pl.num_programs(0)
    c = pl.program_id(1)
    nc = pl.num_programs(1)
    r, d = x_ref.shape
    ck = vm_ref.shape[1]
    k = 3
    org = NB + vhalo

    zero_pad = jnp.zeros((NB, ck), F32)
    for zb, p_ref, m_ref, n_ref in ((zv, vp_ref, vm_ref, vn_ref), (zg, gp_ref, gm_ref, gn_ref)):
        zb[0:NB, :] = zero_pad
        zb[org + r + vhalo:, :] = zero_pad
        zb[NB:org, :] = jnp.where(j > 0, p_ref[...].astype(F32), 0.0)
        zb[org:org + r, :] = m_ref[...].astype(F32)
        zb[org + r:org + r + vhalo, :] = jnp.where(j < nj - 1, n_ref[...].astype(F32), 0.0)

    t_tile = j * (r // NB)
    dys = (0, 1, 2) if vert else (1,)
    span = TM_CHUNK + 2
    for cb in range(ck // 128):
        lo, hi = cb * 128, (cb + 1) * 128
        wv = {(dy, dx): jnp.broadcast_to(dwv_ref[dy * k + dx:dy * k + dx + 1, lo:hi], (NB, 128))
              for dy in dys for dx in range(k)}
        wg = {(dy, dx): jnp.broadcast_to(dwg_ref[dy * k + dx:dy * k + dx + 1, lo:hi], (NB, 128))
              for dy in dys for dx in range(k)}

        def chunk(q, carry, lo=lo, hi=hi, wv=wv, wg=wg):
            base = pl.multiple_of(q * (TM_CHUNK * NB), TM_CHUNK * NB)
            t0 = t_tile + q * TM_CHUNK
            left_ok = (t0 % period) != 0
            right_ok = ((t0 + TM_CHUNK - 1) % period) != period - 1
            res = []
            for zb, wt in ((zv, wv), (zg, wg)):
                ins = {dy: [zb[pl.ds(org + base + (dy - 1) * vhalo + (p - 1) * NB, NB), lo:hi]
                            for p in range(span)] for dy in dys}
                outs = []
                for m in range(TM_CHUNK):
                    def col(dx, m=m, ins=ins, wt=wt):
                        s = None
                        for dy in dys:
                            term = wt[(dy, dx)] * ins[dy][m + dx]
                            s = term if s is None else s + term
                        return s
                    left, mid, right = col(0), col(1), col(2)
                    if m == 0:
                        left = jnp.where(left_ok, left, 0.0)
                    if m == TM_CHUNK - 1:
                        right = jnp.where(right_ok, right, 0.0)
                    outs.append(mid + left + right)
                res.append(jnp.concatenate(outs, axis=0))
            hh[pl.ds(base, TM_CHUNK * NB), lo:hi] = (jax.nn.silu(res[1]) * res[0]).astype(hh.dtype)
            return carry

        lax.fori_loop(0, r // (TM_CHUNK * NB), chunk, 0)

    part = jnp.dot(hh[...], wd_ref[...], preferred_element_type=F32)

    @pl.when(c == 0)
    def _():
        acc[...] = part

    @pl.when(c > 0)
    def _():
        acc[...] += part

    @pl.when(c == nc - 1)
    def _():
        y = _rms(acc[...], gpost_ref[...])
        o_ref[...] = x_ref[...] + _per_batch(y, mod_ref[:, 5 * d:6 * d], None)


def _ffn_back(z, dwf, w_down, x, mod, gpost, tr, ck, vert, period):
    rows, d = x.shape
    f = w_down.shape[0]
    nc = f // ck
    nj = rows // tr
    vhalo = GRID_W * NB if vert else FFN_HALO_ROWS
    assert tr % vhalo == 0 and period % TM_CHUNK == 0
    hb_blocks = tr // vhalo
    last = rows // vhalo - 1
    prev = lambda off: (lambda j, c: (jnp.maximum(j * hb_blocks - 1, 0), c + off))
    main = lambda off: (lambda j, c: (j, c + off))
    nxt = lambda off: (lambda j, c: (jnp.minimum((j + 1) * hb_blocks, last), c + off))
    kern = functools.partial(_ffn_back_kernel, vert=vert, period=period, vhalo=vhalo)
    zrows = tr + 2 * vhalo + 2 * NB
    return pl.pallas_call(
        kern,
        grid=(nj, nc),
        in_specs=[pl.BlockSpec((vhalo, ck), prev(0)), pl.BlockSpec((tr, ck), main(0)),
                  pl.BlockSpec((vhalo, ck), nxt(0)),
                  pl.BlockSpec((vhalo, ck), prev(nc)), pl.BlockSpec((tr, ck), main(nc)),
                  pl.BlockSpec((vhalo, ck), nxt(nc)),
                  pl.BlockSpec((dwf.shape[0], ck), lambda j, c: (0, c)),
                  pl.BlockSpec((dwf.shape[0], ck), lambda j, c: (0, c + nc)),
                  pl.BlockSpec((ck, d), lambda j, c: (c, 0)),
                  pl.BlockSpec((tr, d), lambda j, c: (j, 0)),
                  _const_spec(mod.shape), _const_spec(gpost.shape)],
        out_specs=pl.BlockSpec((tr, d), lambda j, c: (j, 0)),
        out_shape=jax.ShapeDtypeStruct((rows, d), F32),
        scratch_shapes=[pltpu.VMEM((zrows, ck), F32), pltpu.VMEM((zrows, ck), F32),
                        pltpu.VMEM((tr, ck), BF16), pltpu.VMEM((tr, d), F32)],
        compiler_params=_params(("arbitrary", "arbitrary")),
        name="ffn_back",
    )(z, z, z, z, z, z, dwf, dwf, w_down, x, mod, gpost)


def _gate_blocks(w_r, w_i):
    h, hd, _ = w_r.shape
    hpb = GATE_BLK // hd
    nblk = h // hpb
    eye = jnp.eye(hpb, dtype=w_r.dtype)

    def bd(w):
        w = w.reshape(nblk, hpb, hd, hd)
        return (w[:, :, :, None, :] * eye[None, :, None, :, None]).reshape(nblk, GATE_BLK, GATE_BLK)

    return jnp.concatenate([bd(w_r), bd(w_i)], axis=-1).astype(BF16)


def _row_tile(rows, want):
    t = min(want, rows)
    while rows % t:
        t //= 2
    return t


def kernel(x, c, ctx, c_ctx, w_ada, b_ada, g_pre_mix, g_post_mix, g_pre_ffn, g_post_ffn, w_in, dw_conv,
           ln_conv_g, ln_conv_b, w_proj_conv, lru_conv_w, lru_conv_b, w_rgate, b_rgate, w_igate, b_igate,
           lru_lambda, w_proj_lru, w_out, w_up, dw_ffn, w_down):
    b, t_lat, d = x.shape
    t_ctx = ctx.shape[1]
    depth = w_in.shape[0]
    cw = dw_conv.shape[-1]
    lw = lru_conv_w.shape[-1]
    f = w_down.shape[1]
    assert b == NB and t_lat % GRID_W == 0

    xl = jnp.transpose(x, (1, 0, 2)).reshape(t_lat * NB, d)
    xc_ = jnp.transpose(ctx, (1, 0, 2)).reshape(t_ctx * NB, d)

    cc = jnp.zeros((2 * NB, d), F32).at[0:NB].set(c).at[NB].set(c_ctx)
    mod_all = _ada_call(cc, w_ada, b_ada)

    row2 = lambda a: a.reshape(1, -1)
    zeros_state = jnp.zeros((NB, lw), F32)

    for l in range(depth):
        need_ctx = l < depth - 1
        mod_lat = mod_all[l, 0:NB]
        mod_ctx = jnp.broadcast_to(mod_all[l, NB:NB + 1], (NB, mod_all.shape[-1]))
        w_in_l = w_in[l].astype(BF16)
        wg = [_gate_blocks(w_rgate[l, dd], w_igate[l, dd]) for dd in range(2)]
        br = [row2(b_rgate[l, dd]) for dd in range(2)]
        bi = [row2(b_igate[l, dd]) for dd in range(2)]
        lam = [row2(lru_lambda[l, dd]) for dd in range(2)]
        wpc = w_proj_conv[l].astype(BF16)
        wpl = w_proj_lru[l].astype(BF16)
        wo = w_out[l].astype(BF16)
        wup = w_up[l].astype(BF16)
        wdn = w_down[l].astype(BF16)
        dwf = dw_ffn[l].reshape(-1, 2 * f)

        def mixer(xrows, mod, h0_b, h0_f):
            rows = xrows.shape[0]
            act, xr, ggt, smg = _mixer_front(xrows, mod, row2(g_pre_mix[l]), w_in_l, cw, lw,
                                             _row_tile(rows, 512))
            xconv, hbwd, st_b = _scan_bwd(xr, lru_conv_w[l], row2(lru_conv_b[l]), wg[1], br[1], bi[1],
                                          lam[1], h0_b, _row_tile(rows, 512))
            x1, st_f = _mixer_back(act, xconv, hbwd, ggt, smg, xrows, mod, dw_conv[l], row2(ln_conv_g[l]),
                                   row2(ln_conv_b[l]), wpc, wg[0], br[0], bi[0], lam[0], wpl, wo,
                                   row2(g_post_mix[l]), h0_f, _row_tile(rows, 256))
            return x1, st_b, st_f

        def ffn(x1, mod, vert, period, tr):
            z = _ffn_up(x1, mod, row2(g_pre_ffn[l]), wup, _row_tile(x1.shape[0], 512))
            return _ffn_back(z, dwf, wdn, x1, mod, row2(g_post_ffn[l]), tr, 512, vert, period)

        c1, st_b, st_f = mixer(xc_, mod_ctx, zeros_state, zeros_state)
        x1, _, _ = mixer(xl, mod_lat, st_b, st_f)
        xl = ffn(x1, mod_lat, True, GRID_W, 2 * GRID_W * NB)
        if need_ctx:
            xc_ = ffn(c1, mod_ctx, False, t_ctx, _row_tile(t_ctx * NB, 2 * GRID_W * NB))

    return jnp.transpose(xl.reshape(t_lat, NB, d), (1, 0, 2))
```

```python
import functools

import jax
import jax.numpy as jnp
from jax import lax
from jax.experimental import pallas as pl
from jax.experimental.pallas import tpu as pltpu

F32 = jnp.float32
BF16 = jnp.bfloat16

NB = 8
EPS = 1e-6
GRID_W = 64
LRU_C = 8.0
GATE_BLK = 256
V7X_VMEM_LIMIT = 56 * 1024 * 1024
CONV_HALO_ROWS = 128
LRU_HALO_ROWS = 16
TM_CHUNK = 8


def _const_spec(shape):
    nd = len(shape)
    return pl.BlockSpec(shape, lambda *_: (0,) * nd, pipeline_mode=pl.Buffered(1))


def _params(sem):
    return pltpu.CompilerParams(dimension_semantics=sem, vmem_limit_bytes=V7X_VMEM_LIMIT)


def _rms(x, g):
    ms = jnp.mean(x * x, axis=-1, keepdims=True)
    return x * lax.rsqrt(ms + EPS) * g


def _per_batch(x, mul, add):
    r, d = x.shape
    y = x.reshape(r // NB, NB, d) * mul[None]
    if add is not None:
        y = y + add[None]
    return y.reshape(r, d)


def _sigmoid(x):
    return 0.5 * jnp.tanh(0.5 * x) + 0.5


def _silu(x):
    return x * _sigmoid(x)


def _softplus(z):
    return jnp.maximum(z, 0.0) + jnp.log1p(jnp.exp(-jnp.abs(z)))


def _ada_kernel(c_ref, w_ref, b_ref, o_ref):
    s = jax.nn.silu(c_ref[...])
    o_ref[0] = jnp.dot(s, w_ref[0], preferred_element_type=F32) + b_ref[0]


def _ada_call(cc, w_ada, b_ada):
    depth, d, n = w_ada.shape
    tn = 1024 if n % 1024 == 0 else n
    return pl.pallas_call(
        _ada_kernel,
        grid=(depth, n // tn),
        in_specs=[pl.BlockSpec(cc.shape, lambda l, j: (0, 0)),
                  pl.BlockSpec((1, d, tn), lambda l, j: (l, 0, j)),
                  pl.BlockSpec((1, 1, tn), lambda l, j: (l, 0, j))],
        out_specs=pl.BlockSpec((1, cc.shape[0], tn), lambda l, j: (l, 0, j)),
        out_shape=jax.ShapeDtypeStruct((depth, cc.shape[0], n), F32),
        compiler_params=_params(("arbitrary", "arbitrary")),
        name="ada_mod",
    )(cc, w_ada, b_ada.reshape(depth, 1, n))


def _mixer_front_kernel(x_ref, mod_ref, g_ref, w_ref, act_ref, xr_ref, ggt_ref, smg_ref, *, cw, lw):
    r, d = x_ref.shape
    h = _rms(x_ref[...], g_ref[...])
    h = _per_batch(h, 1.0 + mod_ref[:, d:2 * d], mod_ref[:, 0:d])
    hb = h.astype(BF16)

    def mm(lo, hi):
        return jnp.dot(hb, w_ref[:, lo:hi], preferred_element_type=F32)

    o = 0
    v = mm(o, o + cw)
    g = mm(o + cw, o + 2 * cw)
    act_ref[...] = (v * _sigmoid(g)).astype(act_ref.dtype)
    o += 2 * cw
    xr_ref[...] = mm(o, o + lw)
    o += lw
    ggt_ref[...] = jax.nn.gelu(mm(o, o + lw)).astype(ggt_ref.dtype)
    o += lw
    smg_ref[:, 0:d] = _sigmoid(mm(o, o + d)).astype(smg_ref.dtype)
    smg_ref[:, d:2 * d] = _sigmoid(mm(o + d, o + 2 * d)).astype(smg_ref.dtype)


def _mixer_front(x, mod, g_pre, w_in, cw, lw, tr):
    rows, d = x.shape
    kern = functools.partial(_mixer_front_kernel, cw=cw, lw=lw)
    row = lambda i: (i, 0)
    return pl.pallas_call(
        kern,
        grid=(rows // tr,),
        in_specs=[pl.BlockSpec((tr, d), row), _const_spec(mod.shape), _const_spec(g_pre.shape),
                  _const_spec(w_in.shape)],
        out_specs=[pl.BlockSpec((tr, cw), row), pl.BlockSpec((tr, lw), row),
                   pl.BlockSpec((tr, lw), row), pl.BlockSpec((tr, 2 * d), row)],
        out_shape=[jax.ShapeDtypeStruct((rows, cw), BF16), jax.ShapeDtypeStruct((rows, lw), F32),
                   jax.ShapeDtypeStruct((rows, lw), BF16), jax.ShapeDtypeStruct((rows, 2 * d), BF16)],
        compiler_params=_params(("arbitrary",)),
        name="mixer_front",
    )(x, mod, g_pre, w_in)


def _lru_coeffs(xc, wg_ref, br_ref, bi_ref, lam_ref, a_s, u_s):
    xb = xc.astype(BF16)
    nblk = wg_ref.shape[0]
    for blk in range(nblk):
        lo, hi = blk * GATE_BLK, (blk + 1) * GATE_BLK
        g = jnp.dot(xb[:, lo:hi], wg_ref[blk], preferred_element_type=F32)
        rg = _sigmoid(g[:, :GATE_BLK] + br_ref[:, lo:hi])
        ig = _sigmoid(g[:, GATE_BLK:] + bi_ref[:, lo:hi])
        log_a = (-LRU_C * _softplus(-lam_ref[:, lo:hi])) * rg
        a_s[:, lo:hi] = jnp.exp(log_a)
        th = jnp.tanh(log_a)
        u_s[:, lo:hi] = jnp.sqrt(-2.0 * th / (1.0 - th)) * (ig * xc[:, lo:hi])


def _scan_bwd_kernel(xp_ref, xm_ref, xn_ref, cw_ref, cb_ref, wg_ref, br_ref, bi_ref, lam_ref, h0_ref,
                     xc_ref, hb_ref, st_ref, buf, a_s, u_s, h_s):
    i = pl.program_id(0)
    n = pl.num_programs(0)
    j = n - 1 - i
    r = xm_ref.shape[0]
    k_taps = cw_ref.shape[0]

    @pl.when(i == 0)
    def _():
        h_s[...] = h0_ref[...]

    buf[0:NB, :] = jnp.where(j > 0, xp_ref[LRU_HALO_ROWS - NB:, :], 0.0)
    buf[NB:NB + r, :] = xm_ref[...]
    buf[NB + r:, :] = jnp.where(j < n - 1, xn_ref[...], 0.0)
    xc = cb_ref[...] + cw_ref[0:1, :] * buf[0:r, :]
    for k in range(1, k_taps):
        xc = xc + cw_ref[k:k + 1, :] * buf[k * NB:k * NB + r, :]
    xc_ref[...] = xc
    _lru_coeffs(xc, wg_ref, br_ref, bi_ref, lam_ref, a_s, u_s)

    nt = r // NB

    def step(s, h):
        r0 = pl.multiple_of((nt - 1 - s) * NB, NB)
        h = a_s[pl.ds(r0, NB), :] * h + u_s[pl.ds(r0, NB), :]
        hb_ref[pl.ds(r0, NB), :] = h
        return h

    h = lax.fori_loop(0, nt, step, h_s[...], unroll=8)
    h_s[...] = h
    st_ref[...] = h


def _scan_bwd(xr, conv_w, conv_b, wg, br, bi, lam, h0, tr):
    rows, w = xr.shape
    n = rows // tr
    hb = tr // LRU_HALO_ROWS
    last = rows // LRU_HALO_ROWS - 1
    return pl.pallas_call(
        _scan_bwd_kernel,
        grid=(n,),
        in_specs=[pl.BlockSpec((LRU_HALO_ROWS, w), lambda i: (jnp.maximum((n - 1 - i) * hb - 1, 0), 0)),
                  pl.BlockSpec((tr, w), lambda i: (n - 1 - i, 0)),
                  pl.BlockSpec((LRU_HALO_ROWS, w), lambda i: (jnp.minimum((n - i) * hb, last), 0)),
                  _const_spec(conv_w.shape), _const_spec(conv_b.shape), _const_spec(wg.shape),
                  _const_spec(br.shape), _const_spec(bi.shape), _const_spec(lam.shape),
                  _const_spec(h0.shape)],
        out_specs=[pl.BlockSpec((tr, w), lambda i: (n - 1 - i, 0)),
                   pl.BlockSpec((tr, w), lambda i: (n - 1 - i, 0)),
                   pl.BlockSpec((NB, w), lambda i: (0, 0))],
        out_shape=[jax.ShapeDtypeStruct((rows, w), F32), jax.ShapeDtypeStruct((rows, w), F32),
                   jax.ShapeDtypeStruct((NB, w), F32)],
        scratch_shapes=[pltpu.VMEM((tr + NB + LRU_HALO_ROWS, w), F32), pltpu.VMEM((tr, w), F32),
                        pltpu.VMEM((tr, w), F32), pltpu.VMEM((NB, w), F32)],
        compiler_params=_params(("arbitrary",)),
        name="lru_scan_bwd",
    )(xr, xr, xr, conv_w, conv_b, wg, br, bi, lam, h0)


def _dwconv_time(src, dst, w_ref, rows, k_taps):
    lanes = dst.shape[1]
    span = TM_CHUNK + k_taps - 1
    for cb in range(lanes // 128):
        lo, hi = cb * 128, (cb + 1) * 128
        wk = [jnp.broadcast_to(w_ref[k:k + 1, lo:hi], (NB, 128)) for k in range(k_taps)]

        def chunk(c, carry, lo=lo, hi=hi, wk=wk):
            base = pl.multiple_of(c * (TM_CHUNK * NB), TM_CHUNK * NB)
            ins = [src[pl.ds(base + q * NB, NB), lo:hi] for q in range(span)]
            outs = []
            for m in range(TM_CHUNK):
                acc = wk[0] * ins[m]
                for k in range(1, k_taps):
                    acc = acc + wk[k] * ins[m + k]
                outs.append(acc)
            dst[pl.ds(base, TM_CHUNK * NB), lo:hi] = jnp.concatenate(outs, axis=0)
            return carry

        lax.fori_loop(0, rows // (TM_CHUNK * NB), chunk, 0, unroll=2)


def _mixer_back_kernel(ap_ref, am_ref, an_ref, xc_ref, hb_ref, ggt_ref, smg_ref, x_ref, mod_ref,
                       dw_ref, lng_ref, lnb_ref, wpc_ref, wg_ref, br_ref, bi_ref, lam_ref, wpl_ref,
                       wo_ref, gpost_ref, h0_ref, x1_ref, st_ref, cbuf, cout, a_s, u_s, h_s):
    i = pl.program_id(0)
    n = pl.num_programs(0)
    r, d = x_ref.shape
    k_taps = dw_ref.shape[0]
    halo = (k_taps // 2) * NB

    @pl.when(i == 0)
    def _():
        h_s[...] = h0_ref[...]

    prev = ap_ref[...].astype(F32)
    nxt = an_ref[...].astype(F32)
    cbuf[0:halo, :] = jnp.where(i > 0, prev[CONV_HALO_ROWS - halo:, :], 0.0)
    cbuf[halo:halo + r, :] = am_ref[...].astype(F32)
    cbuf[halo + r:, :] = jnp.where(i < n - 1, nxt[0:halo, :], 0.0)
    _dwconv_time(cbuf, cout, dw_ref, r, k_taps)
    u = cout[...]
    mu = jnp.mean(u, axis=-1, keepdims=True)
    uc = u - mu
    var = jnp.mean(uc * uc, axis=-1, keepdims=True)
    ln = uc * lax.rsqrt(var + EPS) * lng_ref[...] + lnb_ref[...]
    y_a = jnp.dot(_silu(ln).astype(BF16), wpc_ref[...], preferred_element_type=F32)

    _lru_coeffs(xc_ref[...], wg_ref, br_ref, bi_ref, lam_ref, a_s, u_s)
    nt = r // NB

    def step(s, h):
        r0 = pl.multiple_of(s * NB, NB)
        h = a_s[pl.ds(r0, NB), :] * h + u_s[pl.ds(r0, NB), :]
        cout[pl.ds(r0, NB), :] = h + hb_ref[pl.ds(r0, NB), :]
        return h

    h = lax.fori_loop(0, nt, step, h_s[...], unroll=8)
    h_s[...] = h
    st_ref[...] = h
    rec = cout[...]
    y_b = jnp.dot((rec * ggt_ref[...].astype(F32)).astype(BF16), wpl_ref[...], preferred_element_type=F32)

    m = smg_ref[:, 0:d].astype(F32) * y_a + smg_ref[:, d:2 * d].astype(F32) * y_b
    y = jnp.dot(m.astype(BF16), wo_ref[...], preferred_element_type=F32)
    x1_ref[...] = x_ref[...] + _per_batch(_rms(y, gpost_ref[...]), mod_ref[:, 2 * d:3 * d], None)


def _mixer_back(act, xc, hb, ggt, smg, x, mod, dw, lng, lnb, wpc, wg, br, bi, lam, wpl, wo, gpost, h0, tr):
    rows, d = x.shape
    cw = act.shape[1]
    w = xc.shape[1]
    assert cw == w, "conv and LRU branch widths share one scratch buffer"
    n = rows // tr
    hb_blocks = tr // CONV_HALO_ROWS
    last = rows // CONV_HALO_ROWS - 1
    k_taps = dw.shape[0]
    halo = (k_taps // 2) * NB
    assert halo <= CONV_HALO_ROWS
    row = lambda i: (i, 0)
    consts = [mod, dw, lng, lnb, wpc, wg, br, bi, lam, wpl, wo, gpost, h0]
    return pl.pallas_call(
        _mixer_back_kernel,
        grid=(n,),
        in_specs=[pl.BlockSpec((CONV_HALO_ROWS, cw), lambda i: (jnp.maximum(i * hb_blocks - 1, 0), 0)),
                  pl.BlockSpec((tr, cw), row),
                  pl.BlockSpec((CONV_HALO_ROWS, cw), lambda i: (jnp.minimum((i + 1) * hb_blocks, last), 0)),
                  pl.BlockSpec((tr, w), row), pl.BlockSpec((tr, w), row), pl.BlockSpec((tr, w), row),
                  pl.BlockSpec((tr, 2 * d), row), pl.BlockSpec((tr, d), row)]
                 + [_const_spec(a.shape) for a in consts],
        out_specs=[pl.BlockSpec((tr, d), row), pl.BlockSpec((NB, w), lambda i: (0, 0))],
        out_shape=[jax.ShapeDtypeStruct((rows, d), F32), jax.ShapeDtypeStruct((NB, w), F32)],
        scratch_shapes=[pltpu.VMEM((tr + 2 * halo, cw), F32), pltpu.VMEM((tr, cw), F32),
                        pltpu.VMEM((tr, w), F32), pltpu.VMEM((tr, w), F32), pltpu.VMEM((NB, w), F32)],
        compiler_params=_params(("arbitrary",)),
        name="mixer_back",
    )(act, act, act, xc, hb, ggt, smg, x, *consts)


def _ffn_up_kernel(x_ref, mod_ref, g_ref, w_ref, z_ref, *, tn):
    r, d = x_ref.shape
    h = _rms(x_ref[...], g_ref[...])
    h = _per_batch(h, 1.0 + mod_ref[:, 4 * d:5 * d], mod_ref[:, 3 * d:4 * d])
    hb = h.astype(BF16)
    for c in range(w_ref.shape[1] // tn):
        z_ref[:, c * tn:(c + 1) * tn] = jnp.dot(
            hb, w_ref[:, c * tn:(c + 1) * tn], preferred_element_type=F32).astype(z_ref.dtype)


def _ffn_up(x, mod, g_pre, w_up, tr):
    rows, d = x.shape
    n2 = w_up.shape[1]
    tn = 1280 if n2 % 1280 == 0 else n2
    row = lambda i: (i, 0)
    return pl.pallas_call(
        functools.partial(_ffn_up_kernel, tn=tn),
        grid=(rows // tr,),
        in_specs=[pl.BlockSpec((tr, d), row), _const_spec(mod.shape), _const_spec(g_pre.shape),
                  _const_spec(w_up.shape)],
        out_specs=pl.BlockSpec((tr, n2), row),
        out_shape=jax.ShapeDtypeStruct((rows, n2), F32),
        compiler_params=_params(("arbitrary",)),
        name="ffn_up",
    )(x, mod, g_pre, w_up)


def _ffn_back_kernel(vp_ref, vm_ref, vn_ref, gp_ref, gm_ref, gn_ref, dwv_ref, dwg_ref, wd_ref, x_ref,
                     mod_ref, gpost_ref, o_ref, hh, acc, *, vert, period):
    j = pl.program_id(0)
    nj = pl.num_programs(0)
    c = pl.program_id(1)
    nc = pl.num_programs(1)
    r, d = x_ref.shape
    ck = vm_ref.shape[1]
    k = 3
    slab = GRID_W * NB if vert else r
    nslab = r // slab
    nchunk = slab // (TM_CHUNK * NB)
    dys = (0, 1, 2) if vert else (1,)
    not_first = j > 0
    not_last = j < nj - 1

    for cb in range(ck // 128):
        lo, hi = cb * 128, (cb + 1) * 128

        def taps(w_ref, lo=lo, hi=hi):
            return {(dy, dx): jnp.broadcast_to(w_ref[dy * k + dx:dy * k + dx + 1, lo:hi], (NB, 128))
                    for dy in dys for dx in range(k)}

        for s in range(nslab):
            arrays = []
            for p_ref, m_ref, n_ref, w_ref in ((vp_ref, vm_ref, vn_ref, dwv_ref), (gp_ref, gm_ref, gn_ref, dwg_ref)):
                wt = taps(w_ref)
                if vert:
                    src = {0: (m_ref, (s - 1) * slab) if s > 0 else (p_ref, 0),
                           1: (m_ref, s * slab),
                           2: (m_ref, (s + 1) * slab) if s < nslab - 1 else (n_ref, 0)}
                    if s == 0:
                        wt.update({(0, dx): jnp.where(not_first, wt[(0, dx)], 0.0) for dx in range(k)})
                    if s == nslab - 1:
                        wt.update({(2, dx): jnp.where(not_last, wt[(2, dx)], 0.0) for dx in range(k)})
                else:
                    src = {1: (m_ref, 0)}
                arrays.append((src, wt, p_ref, n_ref))

            def chunk(q, carry, lo=lo, hi=hi, s=s, arrays=arrays):
                base = pl.multiple_of(q * (TM_CHUNK * NB), TM_CHUNK * NB)
                t0 = j * (r // NB) + s * (slab // NB) + q * TM_CHUNK
                left_ok = (t0 % period) != 0
                right_ok = ((t0 + TM_CHUNK - 1) % period) != period - 1
                row_l = pl.multiple_of(jnp.maximum(base - NB, 0), NB)
                row_r = pl.multiple_of(jnp.minimum(base + TM_CHUNK * NB, slab - NB), NB)
                res = []
                for src, wt, p_ref, n_ref in arrays:
                    ins = {}
                    for dy in dys:
                        ref, off = src[dy]
                        left_in = ref[pl.ds(off + row_l, NB), lo:hi]
                        right_in = ref[pl.ds(off + row_r, NB), lo:hi]
                        if not vert:
                            left_in = jnp.where(q == 0, p_ref[:, lo:hi], left_in)
                            right_in = jnp.where(q == nchunk - 1, n_ref[:, lo:hi], right_in)
                        mid_in = ref[pl.ds(off + base, TM_CHUNK * NB), lo:hi]
                        ins[dy] = ([left_in] + [mid_in[p * NB:(p + 1) * NB] for p in range(TM_CHUNK)]
                                   + [right_in])
                    outs = []
                    for m in range(TM_CHUNK):
                        def col(dx, m=m, ins=ins, wt=wt):
                            acc_ = None
                            for dy in dys:
                                term = wt[(dy, dx)] * ins[dy][m + dx]
                                acc_ = term if acc_ is None else acc_ + term
                            return acc_
                        left, mid, right = col(0), col(1), col(2)
                        if m == 0:
                            left = jnp.where(left_ok, left, 0.0)
                        if m == TM_CHUNK - 1:
                            right = jnp.where(right_ok, right, 0.0)
                        outs.append(mid + left + right)
                    res.append(jnp.concatenate(outs, axis=0))
                hh[pl.ds(s * slab + base, TM_CHUNK * NB), lo:hi] = (_silu(res[1]) * res[0]).astype(hh.dtype)
                return carry

            lax.fori_loop(0, nchunk, chunk, 0, unroll=2)

    part = jnp.dot(hh[...], wd_ref[...], preferred_element_type=F32)

    @pl.when(c == 0)
    def _():
        acc[...] = part

    @pl.when(c > 0)
    def _():
        acc[...] += part

    @pl.when(c == nc - 1)
    def _():
        y = _rms(acc[...], gpost_ref[...])
        o_ref[...] = x_ref[...] + _per_batch(y, mod_ref[:, 5 * d:6 * d], None)


def _ffn_back(z, dwf, w_down, x, mod, gpost, tr, ck, vert, period):
    rows, d = x.shape
    f = w_down.shape[0]
    nc = f // ck
    nj = rows // tr
    vhalo = GRID_W * NB if vert else NB
    assert tr % vhalo == 0 and period % TM_CHUNK == 0 and tr % (TM_CHUNK * NB) == 0
    hb_blocks = tr // vhalo
    last = rows // vhalo - 1
    prev = lambda off: (lambda j, c: (jnp.maximum(j * hb_blocks - 1, 0), c + off))
    main = lambda off: (lambda j, c: (j, c + off))
    nxt = lambda off: (lambda j, c: (jnp.minimum((j + 1) * hb_blocks, last), c + off))
    kern = functools.partial(_ffn_back_kernel, vert=vert, period=period)
    return pl.pallas_call(
        kern,
        grid=(nj, nc),
        in_specs=[pl.BlockSpec((vhalo, ck), prev(0)), pl.BlockSpec((tr, ck), main(0)),
                  pl.BlockSpec((vhalo, ck), nxt(0)),
                  pl.BlockSpec((vhalo, ck), prev(nc)), pl.BlockSpec((tr, ck), main(nc)),
                  pl.BlockSpec((vhalo, ck), nxt(nc)),
                  pl.BlockSpec((dwf.shape[0], ck), lambda j, c: (0, c)),
                  pl.BlockSpec((dwf.shape[0], ck), lambda j, c: (0, c + nc)),
                  pl.BlockSpec((ck, d), lambda j, c: (c, 0)),
                  pl.BlockSpec((tr, d), lambda j, c: (j, 0)),
                  _const_spec(mod.shape), _const_spec(gpost.shape)],
        out_specs=pl.BlockSpec((tr, d), lambda j, c: (j, 0)),
        out_shape=jax.ShapeDtypeStruct((rows, d), F32),
        scratch_shapes=[pltpu.VMEM((tr, ck), BF16), pltpu.VMEM((tr, d), F32)],
        compiler_params=_params(("arbitrary", "arbitrary")),
        name="ffn_back",
    )(z, z, z, z, z, z, dwf, dwf, w_down, x, mod, gpost)


def _gate_blocks(w_r, w_i):
    h, hd, _ = w_r.shape
    hpb = GATE_BLK // hd
    nblk = h // hpb
    eye = jnp.eye(hpb, dtype=w_r.dtype)

    def bd(w):
        w = w.reshape(nblk, hpb, hd, hd)
        return (w[:, :, :, None, :] * eye[None, :, None, :, None]).reshape(nblk, GATE_BLK, GATE_BLK)

    return jnp.concatenate([bd(w_r), bd(w_i)], axis=-1).astype(BF16)


def _row_tile(rows, want):
    t = min(want, rows)
    while rows % t:
        t //= 2
    return t


def kernel(x, c, ctx, c_ctx, w_ada, b_ada, g_pre_mix, g_post_mix, g_pre_ffn, g_post_ffn, w_in, dw_conv,
           ln_conv_g, ln_conv_b, w_proj_conv, lru_conv_w, lru_conv_b, w_rgate, b_rgate, w_igate, b_igate,
           lru_lambda, w_proj_lru, w_out, w_up, dw_ffn, w_down):
    b, t_lat, d = x.shape
    t_ctx = ctx.shape[1]
    depth = w_in.shape[0]
    cw = dw_conv.shape[-1]
    lw = lru_conv_w.shape[-1]
    f = w_down.shape[1]
    assert b == NB and t_lat % GRID_W == 0

    xl = jnp.transpose(x, (1, 0, 2)).reshape(t_lat * NB, d)
    xc_ = jnp.transpose(ctx, (1, 0, 2)).reshape(t_ctx * NB, d)

    cc = jnp.zeros((2 * NB, d), F32).at[0:NB].set(c).at[NB].set(c_ctx)
    mod_all = _ada_call(cc, w_ada, b_ada)

    row2 = lambda a: a.reshape(1, -1)
    zeros_state = jnp.zeros((NB, lw), F32)

    for l in range(depth):
        need_ctx = l < depth - 1
        mod_lat = mod_all[l, 0:NB]
        mod_ctx = jnp.broadcast_to(mod_all[l, NB:NB + 1], (NB, mod_all.shape[-1]))
        w_in_l = w_in[l].astype(BF16)
        wg = [_gate_blocks(w_rgate[l, dd], w_igate[l, dd]) for dd in range(2)]
        br = [row2(b_rgate[l, dd]) for dd in range(2)]
        bi = [row2(b_igate[l, dd]) for dd in range(2)]
        lam = [row2(lru_lambda[l, dd]) for dd in range(2)]
        wpc = w_proj_conv[l].astype(BF16)
        wpl = w_proj_lru[l].astype(BF16)
        wo = w_out[l].astype(BF16)
        wup = w_up[l].astype(BF16)
        wdn = w_down[l].astype(BF16)
        dwf = dw_ffn[l].reshape(-1, 2 * f)

        def mixer(xrows, mod, h0_b, h0_f):
            rows = xrows.shape[0]
            act, xr, ggt, smg = _mixer_front(xrows, mod, row2(g_pre_mix[l]), w_in_l, cw, lw,
                                             _row_tile(rows, 512))
            xconv, hbwd, st_b = _scan_bwd(xr, lru_conv_w[l], row2(lru_conv_b[l]), wg[1], br[1], bi[1],
                                          lam[1], h0_b, _row_tile(rows, 512))
            x1, st_f = _mixer_back(act, xconv, hbwd, ggt, smg, xrows, mod, dw_conv[l], row2(ln_conv_g[l]),
                                   row2(ln_conv_b[l]), wpc, wg[0], br[0], bi[0], lam[0], wpl, wo,
                                   row2(g_post_mix[l]), h0_f, _row_tile(rows, 512))
            return x1, st_b, st_f

        def ffn(x1, mod, vert, period, tr):
            z = _ffn_up(x1, mod, row2(g_pre_ffn[l]), wup, _row_tile(x1.shape[0], 512))
            return _ffn_back(z, dwf, wdn, x1, mod, row2(g_post_ffn[l]), tr, 512, vert, period)

        c1, st_b, st_f = mixer(xc_, mod_ctx, zeros_state, zeros_state)
        x1, _, _ = mixer(xl, mod_lat, st_b, st_f)
        xl = ffn(x1, mod_lat, True, GRID_W, 2 * GRID_W * NB)
        if need_ctx:
            xc_ = ffn(c1, mod_ctx, False, t_ctx, _row_tile(t_ctx * NB, 2 * GRID_W * NB))

    return jnp.transpose(xl.reshape(t_lat, NB, d), (1, 0, 2))
```

```python
import functools

import jax
import jax.numpy as jnp
from jax import lax
from jax.experimental import pallas as pl
from jax.experimental.pallas import tpu as pltpu

F32 = jnp.float32
BF16 = jnp.bfloat16

NB = 8
EPS = 1e-6
GRID_W = 64
LRU_C = 8.0
GATE_BLK = 256
V7X_VMEM_LIMIT = 56 * 1024 * 1024
CONV_HALO_ROWS = 128
LRU_HALO_ROWS = 16
FFN_TILE_STEPS = GRID_W
FFN_COLS = 256
FFN_DOT_ROWS = 128
FFN_CHANNEL_BLOCKS = 2
TM_CHUNK = 8
MIX_TILE_ROWS = 512


def _const_spec(shape):
    nd = len(shape)
    return pl.BlockSpec(shape, lambda *_: (0,) * nd, pipeline_mode=pl.Buffered(1))


def _params(sem):
    return pltpu.CompilerParams(dimension_semantics=sem, vmem_limit_bytes=V7X_VMEM_LIMIT)


def _rms(x, g):
    ms = jnp.mean(x * x, axis=-1, keepdims=True)
    return x * lax.rsqrt(ms + EPS) * g


def _per_batch(x, mul, add):
    r, d = x.shape
    y = x.reshape(r // NB, NB, d) * mul[None]
    if add is not None:
        y = y + add[None]
    return y.reshape(r, d)


def _sigmoid(x):
    return 0.5 * jnp.tanh(0.5 * x) + 0.5


def _silu(x):
    return x * _sigmoid(x)


def _softplus(z):
    return jnp.maximum(z, 0.0) + jnp.log1p(jnp.exp(-jnp.abs(z)))


def _interleave(stages):
    for mxu_tasks, vpu_tasks in stages:
        mxu_tasks, vpu_tasks = list(mxu_tasks), list(vpu_tasks)
        per = -(-len(vpu_tasks) // max(len(mxu_tasks), 1))
        while mxu_tasks or vpu_tasks:
            if mxu_tasks:
                mxu_tasks.pop(0)()
            for _ in range(per):
                if vpu_tasks:
                    vpu_tasks.pop(0)()


def _ada_kernel(c_ref, w_ref, b_ref, o_ref):
    s = jax.nn.silu(c_ref[...])
    o_ref[0] = jnp.dot(s, w_ref[0], preferred_element_type=F32) + b_ref[0]


def _ada_call(cc, w_ada, b_ada):
    depth, d, n = w_ada.shape
    tn = 1024 if n % 1024 == 0 else n
    return pl.pallas_call(
        _ada_kernel,
        grid=(depth, n // tn),
        in_specs=[pl.BlockSpec(cc.shape, lambda l, j: (0, 0)),
                  pl.BlockSpec((1, d, tn), lambda l, j: (l, 0, j)),
                  pl.BlockSpec((1, 1, tn), lambda l, j: (l, 0, j))],
        out_specs=pl.BlockSpec((1, cc.shape[0], tn), lambda l, j: (l, 0, j)),
        out_shape=jax.ShapeDtypeStruct((depth, cc.shape[0], n), F32),
        compiler_params=_params(("arbitrary", "arbitrary")),
        name="ada_mod",
    )(cc, w_ada, b_ada.reshape(depth, 1, n))


def _mixer_front_kernel(x_ref, mod_ref, g_ref, w_ref, act_ref, xr_ref, ggt_ref, smg_ref, *, cw, lw):
    r, d = x_ref.shape
    h = _rms(x_ref[...], g_ref[...])
    h = _per_batch(h, 1.0 + mod_ref[:, d:2 * d], mod_ref[:, 0:d])
    hb = h.astype(BF16)

    def mm(lo, hi):
        return jnp.dot(hb, w_ref[:, lo:hi], preferred_element_type=F32)

    o = 0
    v = mm(o, o + cw)
    g = mm(o + cw, o + 2 * cw)
    act_ref[...] = (v * _sigmoid(g)).astype(act_ref.dtype)
    o += 2 * cw
    xr_ref[...] = mm(o, o + lw)
    o += lw
    ggt_ref[...] = jax.nn.gelu(mm(o, o + lw)).astype(ggt_ref.dtype)
    o += lw
    smg_ref[:, 0:d] = _sigmoid(mm(o, o + d)).astype(smg_ref.dtype)
    smg_ref[:, d:2 * d] = _sigmoid(mm(o + d, o + 2 * d)).astype(smg_ref.dtype)


def _mixer_front(x, mod, g_pre, w_in, cw, lw, tr):
    rows, d = x.shape
    kern = functools.partial(_mixer_front_kernel, cw=cw, lw=lw)
    row = lambda i: (i, 0)
    return pl.pallas_call(
        kern,
        grid=(rows // tr,),
        in_specs=[pl.BlockSpec((tr, d), row), _const_spec(mod.shape), _const_spec(g_pre.shape),
                  _const_spec(w_in.shape)],
        out_specs=[pl.BlockSpec((tr, cw), row), pl.BlockSpec((tr, lw), row),
                   pl.BlockSpec((tr, lw), row), pl.BlockSpec((tr, 2 * d), row)],
        out_shape=[jax.ShapeDtypeStruct((rows, cw), BF16), jax.ShapeDtypeStruct((rows, lw), F32),
                   jax.ShapeDtypeStruct((rows, lw), BF16), jax.ShapeDtypeStruct((rows, 2 * d), BF16)],
        compiler_params=_params(("arbitrary",)),
        name="mixer_front",
    )(x, mod, g_pre, w_in)


def _lru_coeffs(xc, wg_ref, br_ref, bi_ref, lam_ref, a_s, u_s):
    xb = xc.astype(BF16)
    nblk = wg_ref.shape[0]
    for blk in range(nblk):
        lo, hi = blk * GATE_BLK, (blk + 1) * GATE_BLK
        g = jnp.dot(xb[:, lo:hi], wg_ref[blk], preferred_element_type=F32)
        t_r = jnp.tanh(g[:, :GATE_BLK] + 0.5 * br_ref[:, lo:hi])
        t_i = jnp.tanh(g[:, GATE_BLK:] + 0.5 * bi_ref[:, lo:hi])
        hc = (-0.5 * LRU_C) * _softplus(-lam_ref[:, lo:hi])
        log_a = hc * t_r + hc
        a_s[:, lo:hi] = jnp.exp(log_a)
        th = jnp.tanh(log_a)
        u_s[:, lo:hi] = jnp.sqrt(-0.5 * th) * lax.rsqrt(1.0 - th) * (t_i + 1.0) * xc[:, lo:hi]


def _scan_bwd_kernel(xp_ref, xm_ref, xn_ref, cw_ref, cb_ref, wg_ref, br_ref, bi_ref, lam_ref, h0_ref,
                     xc_ref, hb_ref, st_ref, buf, a_s, u_s, h_s):
    i = pl.program_id(0)
    n = pl.num_programs(0)
    j = n - 1 - i
    r = xm_ref.shape[0]
    k_taps = cw_ref.shape[0]

    @pl.when(i == 0)
    def _():
        h_s[...] = h0_ref[...]

    buf[0:NB, :] = jnp.where(j > 0, xp_ref[LRU_HALO_ROWS - NB:, :], 0.0)
    buf[NB:NB + r, :] = xm_ref[...]
    buf[NB + r:, :] = jnp.where(j < n - 1, xn_ref[...], 0.0)
    xc = cb_ref[...] + cw_ref[0:1, :] * buf[0:r, :]
    for k in range(1, k_taps):
        xc = xc + cw_ref[k:k + 1, :] * buf[k * NB:k * NB + r, :]
    xc_ref[...] = xc
    _lru_coeffs(xc, wg_ref, br_ref, bi_ref, lam_ref, a_s, u_s)

    nt = r // NB

    def step(s, h):
        r0 = pl.multiple_of((nt - 1 - s) * NB, NB)
        h = a_s[pl.ds(r0, NB), :] * h + u_s[pl.ds(r0, NB), :]
        hb_ref[pl.ds(r0, NB), :] = h
        return h

    h = lax.fori_loop(0, nt, step, h_s[...], unroll=8)
    h_s[...] = h
    st_ref[...] = h


def _scan_bwd(xr, conv_w, conv_b, wg, br, bi, lam, h0, tr):
    rows, w = xr.shape
    n = rows // tr
    hb = tr // LRU_HALO_ROWS
    last = rows // LRU_HALO_ROWS - 1
    return pl.pallas_call(
        _scan_bwd_kernel,
        grid=(n,),
        in_specs=[pl.BlockSpec((LRU_HALO_ROWS, w), lambda i: (jnp.maximum((n - 1 - i) * hb - 1, 0), 0)),
                  pl.BlockSpec((tr, w), lambda i: (n - 1 - i, 0)),
                  pl.BlockSpec((LRU_HALO_ROWS, w), lambda i: (jnp.minimum((n - i) * hb, last), 0)),
                  _const_spec(conv_w.shape), _const_spec(conv_b.shape), _const_spec(wg.shape),
                  _const_spec(br.shape), _const_spec(bi.shape), _const_spec(lam.shape),
                  _const_spec(h0.shape)],
        out_specs=[pl.BlockSpec((tr, w), lambda i: (n - 1 - i, 0)),
                   pl.BlockSpec((tr, w), lambda i: (n - 1 - i, 0)),
                   pl.BlockSpec((NB, w), lambda i: (0, 0))],
        out_shape=[jax.ShapeDtypeStruct((rows, w), F32), jax.ShapeDtypeStruct((rows, w), F32),
                   jax.ShapeDtypeStruct((NB, w), F32)],
        scratch_shapes=[pltpu.VMEM((tr + NB + LRU_HALO_ROWS, w), F32), pltpu.VMEM((tr, w), F32),
                        pltpu.VMEM((tr, w), F32), pltpu.VMEM((NB, w), F32)],
        compiler_params=_params(("arbitrary",)),
        name="lru_scan_bwd",
    )(xr, xr, xr, conv_w, conv_b, wg, br, bi, lam, h0)


def _dwconv_time(src, dst, w_ref, rows, k_taps):
    lanes = dst.shape[1]
    span = TM_CHUNK + k_taps - 1
    for cb in range(lanes // 128):
        lo, hi = cb * 128, (cb + 1) * 128
        wk = [jnp.broadcast_to(w_ref[k:k + 1, lo:hi], (NB, 128)) for k in range(k_taps)]

        def chunk(c, carry, lo=lo, hi=hi, wk=wk):
            base = pl.multiple_of(c * (TM_CHUNK * NB), TM_CHUNK * NB)
            ins = [src[pl.ds(base + q * NB, NB), lo:hi] for q in range(span)]
            outs = []
            for m in range(TM_CHUNK):
                acc = wk[0] * ins[m]
                for k in range(1, k_taps):
                    acc = acc + wk[k] * ins[m + k]
                outs.append(acc)
            dst[pl.ds(base, TM_CHUNK * NB), lo:hi] = jnp.concatenate(outs, axis=0)
            return carry

        lax.fori_loop(0, rows // (TM_CHUNK * NB), chunk, 0, unroll=2)


def _mixer_back_kernel(ap_ref, am_ref, an_ref, xc_ref, hb_ref, ggt_ref, smg_ref, x_ref, mod_ref,
                       dw_ref, lng_ref, lnb_ref, wpc_ref, wg_ref, br_ref, bi_ref, lam_ref, wpl_ref,
                       wo_ref, gpost_ref, h0_ref, x1_ref, st_ref, cbuf, cout, a_s, u_s, h_s):
    i = pl.program_id(0)
    n = pl.num_programs(0)
    r, d = x_ref.shape
    k_taps = dw_ref.shape[0]
    halo = (k_taps // 2) * NB

    @pl.when(i == 0)
    def _():
        h_s[...] = h0_ref[...]

    prev = ap_ref[...].astype(F32)
    nxt = an_ref[...].astype(F32)
    cbuf[0:halo, :] = jnp.where(i > 0, prev[CONV_HALO_ROWS - halo:, :], 0.0)
    cbuf[halo:halo + r, :] = am_ref[...].astype(F32)
    cbuf[halo + r:, :] = jnp.where(i < n - 1, nxt[0:halo, :], 0.0)
    _dwconv_time(cbuf, cout, dw_ref, r, k_taps)
    u = cout[...]
    mu = jnp.mean(u, axis=-1, keepdims=True)
    uc = u - mu
    var = jnp.mean(uc * uc, axis=-1, keepdims=True)
    ln = uc * lax.rsqrt(var + EPS) * lng_ref[...] + lnb_ref[...]
    y_a = jnp.dot(_silu(ln).astype(BF16), wpc_ref[...], preferred_element_type=F32)

    _lru_coeffs(xc_ref[...], wg_ref, br_ref, bi_ref, lam_ref, a_s, u_s)
    nt = r // NB

    def step(s, h):
        r0 = pl.multiple_of(s * NB, NB)
        h = a_s[pl.ds(r0, NB), :] * h + u_s[pl.ds(r0, NB), :]
        cout[pl.ds(r0, NB), :] = h + hb_ref[pl.ds(r0, NB), :]
        return h

    h = lax.fori_loop(0, nt, step, h_s[...], unroll=8)
    h_s[...] = h
    st_ref[...] = h
    rec = cout[...]
    y_b = jnp.dot((rec * ggt_ref[...].astype(F32)).astype(BF16), wpl_ref[...], preferred_element_type=F32)

    m = smg_ref[:, 0:d].astype(F32) * y_a + smg_ref[:, d:2 * d].astype(F32) * y_b
    y = jnp.dot(m.astype(BF16), wo_ref[...], preferred_element_type=F32)
    x1_ref[...] = x_ref[...] + _per_batch(_rms(y, gpost_ref[...]), mod_ref[:, 2 * d:3 * d], None)


def _mixer_back(act, xc, hb, ggt, smg, x, mod, dw, lng, lnb, wpc, wg, br, bi, lam, wpl, wo, gpost, h0, tr):
    rows, d = x.shape
    cw = act.shape[1]
    w = xc.shape[1]
    assert cw == w, "conv and LRU branch widths share one scratch buffer"
    n = rows // tr
    hb_blocks = tr // CONV_HALO_ROWS
    last = rows // CONV_HALO_ROWS - 1
    k_taps = dw.shape[0]
    halo = (k_taps // 2) * NB
    assert halo <= CONV_HALO_ROWS
    row = lambda i: (i, 0)
    consts = [mod, dw, lng, lnb, wpc, wg, br, bi, lam, wpl, wo, gpost, h0]
    return pl.pallas_call(
        _mixer_back_kernel,
        grid=(n,),
        in_specs=[pl.BlockSpec((CONV_HALO_ROWS, cw), lambda i: (jnp.maximum(i * hb_blocks - 1, 0), 0)),
                  pl.BlockSpec((tr, cw), row),
                  pl.BlockSpec((CONV_HALO_ROWS, cw), lambda i: (jnp.minimum((i + 1) * hb_blocks, last), 0)),
                  pl.BlockSpec((tr, w), row), pl.BlockSpec((tr, w), row), pl.BlockSpec((tr, w), row),
                  pl.BlockSpec((tr, 2 * d), row), pl.BlockSpec((tr, d), row)]
                 + [_const_spec(a.shape) for a in consts],
        out_specs=[pl.BlockSpec((tr, d), row), pl.BlockSpec((NB, w), lambda i: (0, 0))],
        out_shape=[jax.ShapeDtypeStruct((rows, d), F32), jax.ShapeDtypeStruct((NB, w), F32)],
        scratch_shapes=[pltpu.VMEM((tr + 2 * halo, cw), F32), pltpu.VMEM((tr, cw), F32),
                        pltpu.VMEM((tr, w), F32), pltpu.VMEM((tr, w), F32), pltpu.VMEM((NB, w), F32)],
        compiler_params=_params(("arbitrary",)),
        name="mixer_back",
    )(act, act, act, xc, hb, ggt, smg, x, *consts)


def _ffn_up_conv_kernel(x_ref, mod_ref, g_ref, wv_ref, wg_ref, dwv_ref, dwg_ref, hh_ref, zr, *, vert, nt):
    i = pl.program_id(1)
    r, d = x_ref.shape
    f = hh_ref.shape[1]
    k = 3
    nchunk = r // (TM_CHUNK * NB)
    slot_c = 0
    slot_b = 2
    slot_a = 1
    has_before = i > 1
    has_after = i < nt

    @pl.when(i == 0)
    def _():
        zr[1] = jnp.zeros(zr.shape[1:], F32)
        zr[2] = jnp.zeros(zr.shape[1:], F32)

    def conv_tasks(lo):
        arrays = []
        for dw_ref, col in ((dwv_ref, lo), (dwg_ref, f + lo)):
            w = {(dy, dx): jnp.broadcast_to(dw_ref[dy * k + dx:dy * k + dx + 1, lo:lo + 128], (NB, 128))
                 for dy in range(k) for dx in range(k)}
            if vert:
                srcs = ((slot_a, {dx: jnp.where(has_before, w[(0, dx)], 0.0) for dx in range(k)}),
                        (slot_b, {dx: w[(1, dx)] for dx in range(k)}),
                        (slot_c, {dx: jnp.where(has_after, w[(2, dx)], 0.0) for dx in range(k)}))
                edge = None
            else:
                srcs = ((slot_b, {dx: w[(1, dx)] for dx in range(k)}),)
                edge = (jnp.where(has_before, w[(1, 0)], 0.0), jnp.where(has_after, w[(1, 2)], 0.0))
            arrays.append((col, srcs, edge))
        return [functools.partial(conv_chunk, lo, arrays, q) for q in range(nchunk)]

    def conv_chunk(lo, arrays, q):
        base = q * TM_CHUNK * NB
        lo_r = max(base - NB, 0)
        hi_r = min(base + (TM_CHUNK + 1) * NB, r)
        res = []
        for col, srcs, edge in arrays:
            pc, ln = col // FFN_COLS, col % FFN_COLS
            blks = [zr[slot, pc, lo_r:hi_r, ln:ln + 128] for slot, _ in srcs]
            outs = []
            for m in range(TM_CHUNK):
                acc = None
                for (_, wt), blk in zip(srcs, blks):
                    for dx in range(k):
                        row = base + (m + dx - 1) * NB
                        if row < 0 or row >= r:
                            continue
                        term = wt[dx] * blk[row - lo_r:row - lo_r + NB]
                        acc = term if acc is None else acc + term
                if edge is not None and q == 0 and m == 0:
                    acc = acc + edge[0] * zr[slot_a, pc, r - NB:r, ln:ln + 128]
                if edge is not None and q == nchunk - 1 and m == TM_CHUNK - 1:
                    acc = acc + edge[1] * zr[slot_c, pc, 0:NB, ln:ln + 128]
                outs.append(acc)
            res.append(jnp.concatenate(outs, axis=0))
        hh_ref[base:base + TM_CHUNK * NB, lo:lo + 128] = (_silu(res[1]) * res[0]).astype(hh_ref.dtype)

    h = _rms(x_ref[...], g_ref[...])
    h = _per_batch(h, 1.0 + mod_ref[:, 4 * d:5 * d], mod_ref[:, 3 * d:4 * d])
    hb = h.astype(BF16)
    nk = f // FFN_COLS

    def dot_task(w_ref, off, c0, r0):
        zr[slot_c, (off + c0) // FFN_COLS, r0:r0 + FFN_DOT_ROWS] = jnp.dot(
            hb[r0:r0 + FFN_DOT_ROWS], w_ref[:, c0:c0 + FFN_COLS], preferred_element_type=F32)

    def age_task(pc):
        zr[slot_a, pc] = zr[slot_b, pc]
        zr[slot_b, pc] = zr[slot_c, pc]

    stages = []
    for kk in range(nk + 1):
        mxu_tasks, vpu_tasks = [], []
        if kk < nk:
            c0 = kk * FFN_COLS
            mxu_tasks = [functools.partial(dot_task, w_ref, off, c0, r0)
                         for w_ref, off in ((wv_ref, 0), (wg_ref, f)) for r0 in range(0, r, FFN_DOT_ROWS)]
        if kk >= 1:
            c0 = (kk - 1) * FFN_COLS
            for cb in range(FFN_COLS // 128):
                vpu_tasks += conv_tasks(c0 + cb * 128)
            vpu_tasks += [functools.partial(age_task, (off + c0) // FFN_COLS) for off in (0, f)]
        stages.append((mxu_tasks, vpu_tasks))
    _interleave(stages)


def _ffn_up_conv(x, mod, g_pre, w_up, dwf, vert):
    rows, d = x.shape
    f = w_up.shape[1] // 2
    tr = FFN_TILE_STEPS * NB
    nt = rows // tr
    fb = f // FFN_CHANNEL_BLOCKS
    nb = FFN_CHANNEL_BLOCKS
    assert rows % tr == 0 and fb % FFN_COLS == 0 and (not vert or FFN_TILE_STEPS == GRID_W)
    return pl.pallas_call(
        functools.partial(_ffn_up_conv_kernel, vert=vert, nt=nt),
        grid=(nb, nt + 1),
        in_specs=[pl.BlockSpec((tr, d), lambda p, i: (jnp.minimum(i, nt - 1), 0)), _const_spec(mod.shape),
                  _const_spec(g_pre.shape),
                  pl.BlockSpec((d, fb), lambda p, i: (0, p)), pl.BlockSpec((d, fb), lambda p, i: (0, nb + p)),
                  pl.BlockSpec((dwf.shape[0], fb), lambda p, i: (0, p)),
                  pl.BlockSpec((dwf.shape[0], fb), lambda p, i: (0, nb + p))],
        out_specs=pl.BlockSpec((tr, fb), lambda p, i: (jnp.maximum(i - 1, 0), p)),
        out_shape=jax.ShapeDtypeStruct((rows, f), BF16),
        scratch_shapes=[pltpu.VMEM((3, 2 * fb // FFN_COLS, tr, FFN_COLS), F32)],
        compiler_params=_params(("arbitrary", "arbitrary")),
        name="ffn_up_conv",
    )(x, mod, g_pre, w_up, w_up, dwf, dwf)


def _ffn_down_kernel(hh_ref, wd_ref, x_ref, mod_ref, gpost_ref, o_ref):
    d = x_ref.shape[1]
    y = jnp.dot(hh_ref[...], wd_ref[...], preferred_element_type=F32)
    o_ref[...] = x_ref[...] + _per_batch(_rms(y, gpost_ref[...]), mod_ref[:, 5 * d:6 * d], None)


def _ffn_down(hh, w_down, x, mod, gpost, tr):
    rows, d = x.shape
    f = hh.shape[1]
    row = lambda i: (i, 0)
    return pl.pallas_call(
        _ffn_down_kernel,
        grid=(rows // tr,),
        in_specs=[pl.BlockSpec((tr, f), row), _const_spec(w_down.shape), pl.BlockSpec((tr, d), row),
                  _const_spec(mod.shape), _const_spec(gpost.shape)],
        out_specs=pl.BlockSpec((tr, d), row),
        out_shape=jax.ShapeDtypeStruct((rows, d), F32),
        compiler_params=_params(("arbitrary",)),
        name="ffn_down",
    )(hh, w_down, x, mod, gpost)


def _gate_blocks(w_r, w_i):
    h, hd, _ = w_r.shape
    hpb = GATE_BLK // hd
    nblk = h // hpb
    eye = jnp.eye(hpb, dtype=w_r.dtype)

    def bd(w):
        w = w.reshape(nblk, hpb, hd, hd)
        return (w[:, :, :, None, :] * eye[None, :, None, :, None]).reshape(nblk, GATE_BLK, GATE_BLK)

    return (0.5 * jnp.concatenate([bd(w_r), bd(w_i)], axis=-1)).astype(BF16)


def _row_tile(rows, want):
    t = min(want, rows)
    while rows % t:
        t //= 2
    return t


def kernel(x, c, ctx, c_ctx, w_ada, b_ada, g_pre_mix, g_post_mix, g_pre_ffn, g_post_ffn, w_in, dw_conv,
           ln_conv_g, ln_conv_b, w_proj_conv, lru_conv_w, lru_conv_b, w_rgate, b_rgate, w_igate, b_igate,
           lru_lambda, w_proj_lru, w_out, w_up, dw_ffn, w_down):
    b, t_lat, d = x.shape
    t_ctx = ctx.shape[1]
    depth = w_in.shape[0]
    cw = dw_conv.shape[-1]
    lw = lru_conv_w.shape[-1]
    f = w_down.shape[1]
    assert b == NB and t_lat % GRID_W == 0

    xl = jnp.transpose(x, (1, 0, 2)).reshape(t_lat * NB, d)
    xc_ = jnp.transpose(ctx, (1, 0, 2)).reshape(t_ctx * NB, d)

    cc = jnp.zeros((2 * NB, d), F32).at[0:NB].set(c).at[NB].set(c_ctx)
    mod_all = _ada_call(cc, w_ada, b_ada)

    row2 = lambda a: a.reshape(1, -1)
    zeros_state = jnp.zeros((NB, lw), F32)

    for l in range(depth):
        need_ctx = l < depth - 1
        mod_lat = mod_all[l, 0:NB]
        mod_ctx = jnp.broadcast_to(mod_all[l, NB:NB + 1], (NB, mod_all.shape[-1]))
        w_in_l = w_in[l].astype(BF16)
        wg = [_gate_blocks(w_rgate[l, dd], w_igate[l, dd]) for dd in range(2)]
        br = [row2(b_rgate[l, dd]) for dd in range(2)]
        bi = [row2(b_igate[l, dd]) for dd in range(2)]
        lam = [row2(lru_lambda[l, dd]) for dd in range(2)]
        wpc = w_proj_conv[l].astype(BF16)
        wpl = w_proj_lru[l].astype(BF16)
        wo = w_out[l].astype(BF16)
        wup = w_up[l].astype(BF16)
        wdn = w_down[l].astype(BF16)
        dwf = dw_ffn[l].reshape(-1, 2 * f)

        def mixer(xrows, mod, h0_b, h0_f):
            rows = xrows.shape[0]
            act, xr, ggt, smg = _mixer_front(xrows, mod, row2(g_pre_mix[l]), w_in_l, cw, lw,
                                             _row_tile(rows, MIX_TILE_ROWS))
            xconv, hbwd, st_b = _scan_bwd(xr, lru_conv_w[l], row2(lru_conv_b[l]), wg[1], br[1], bi[1],
                                          lam[1], h0_b, _row_tile(rows, MIX_TILE_ROWS))
            x1, st_f = _mixer_back(act, xconv, hbwd, ggt, smg, xrows, mod, dw_conv[l], row2(ln_conv_g[l]),
                                   row2(ln_conv_b[l]), wpc, wg[0], br[0], bi[0], lam[0], wpl, wo,
                                   row2(g_post_mix[l]), h0_f, _row_tile(rows, MIX_TILE_ROWS))
            return x1, st_b, st_f

        def ffn(x1, mod, vert):
            hh = _ffn_up_conv(x1, mod, row2(g_pre_ffn[l]), wup, dwf, vert)
            return _ffn_down(hh, wdn, x1, mod, row2(g_post_ffn[l]), _row_tile(x1.shape[0], MIX_TILE_ROWS))

        c1, st_b, st_f = mixer(xc_, mod_ctx, zeros_state, zeros_state)
        x1, _, _ = mixer(xl, mod_lat, st_b, st_f)
        xl = ffn(x1, mod_lat, True)
        if need_ctx:
            xc_ = ffn(c1, mod_ctx, False)

    return jnp.transpose(xl.reshape(t_lat, NB, d), (1, 0, 2))
```

```python
import functools

import jax
import jax.numpy as jnp
from jax import lax
from jax.experimental import pallas as pl
from jax.experimental.pallas import tpu as pltpu

F32 = jnp.float32
BF16 = jnp.bfloat16

NB = 8
EPS = 1e-6
GRID_W = 64
LRU_C = 8.0
GATE_BLK = 256
V7X_VMEM_LIMIT = 56 * 1024 * 1024
CONV_HALO_ROWS = 128
LRU_HALO_ROWS = 16
FFN_TILE_STEPS = GRID_W
FFN_COLS = 256
FFN_DOT_ROWS = 128
FFN_CHANNEL_BLOCKS = 2
TM_CHUNK = 8
MIX_TILE_ROWS = 512
MIX_PARTS = 2
MIX_COLS = 256
MIX_DOT_ROWS = 256


def _const_spec(shape):
    nd = len(shape)
    return pl.BlockSpec(shape, lambda *_: (0,) * nd, pipeline_mode=pl.Buffered(1))


def _params(sem):
    return pltpu.CompilerParams(dimension_semantics=sem, vmem_limit_bytes=V7X_VMEM_LIMIT)


def _rms(x, g):
    ms = jnp.mean(x * x, axis=-1, keepdims=True)
    return x * lax.rsqrt(ms + EPS) * g


def _per_batch(x, mul, add):
    r, d = x.shape
    y = x.reshape(r // NB, NB, d) * mul[None]
    if add is not None:
        y = y + add[None]
    return y.reshape(r, d)


def _sigmoid(x):
    return 0.5 * jnp.tanh(0.5 * x) + 0.5


def _silu(x):
    return x * _sigmoid(x)


def _softplus(z):
    return jnp.maximum(z, 0.0) + jnp.log1p(jnp.exp(-jnp.abs(z)))


def _interleave(stages):
    for mxu_tasks, vpu_tasks in stages:
        nm, nv = len(mxu_tasks), len(vpu_tasks)
        done = 0
        for k, task in enumerate(mxu_tasks):
            task()
            want = -(-(k + 1) * nv // nm)
            for v in vpu_tasks[done:want]:
                v()
            done = max(done, want)
        for v in vpu_tasks[done:]:
            v()


def _ada_kernel(c_ref, w_ref, b_ref, o_ref):
    s = jax.nn.silu(c_ref[...])
    o_ref[0] = jnp.dot(s, w_ref[0], preferred_element_type=F32) + b_ref[0]


def _ada_call(cc, w_ada, b_ada):
    depth, d, n = w_ada.shape
    tn = 1024 if n % 1024 == 0 else n
    return pl.pallas_call(
        _ada_kernel,
        grid=(depth, n // tn),
        in_specs=[pl.BlockSpec(cc.shape, lambda l, j: (0, 0)),
                  pl.BlockSpec((1, d, tn), lambda l, j: (l, 0, j)),
                  pl.BlockSpec((1, 1, tn), lambda l, j: (l, 0, j))],
        out_specs=pl.BlockSpec((1, cc.shape[0], tn), lambda l, j: (l, 0, j)),
        out_shape=jax.ShapeDtypeStruct((depth, cc.shape[0], n), F32),
        compiler_params=_params(("arbitrary", "arbitrary")),
        name="ada_mod",
    )(cc, w_ada, b_ada.reshape(depth, 1, n))


def _mixer_front_kernel(xp_ref, x_ref, xn_ref, mod_ref, g_ref, w_ref, cw_ref, cb_ref,
                        wgf_ref, brf_ref, bif_ref, lamf_ref, wgb_ref, brb_ref, bib_ref, lamb_ref, h0_ref,
                        act_ref, ggt_ref, smg_ref, af_ref, uf_ref, hb_ref, st_ref,
                        buf, xc_s, ab_s, ub_s, h_s, *, cw, lw):
    step_id = pl.program_id(0)
    n = pl.num_programs(0)
    i = n - 1 - step_id
    r, d = x_ref.shape

    @pl.when(step_id == 0)
    def _():
        h_s[...] = h0_ref[...]

    k_taps = cw_ref.shape[0]
    o_xr, o_gt, o_mg = 2 * cw, 2 * cw + lw, 2 * cw + 2 * lw

    def normed(x):
        h = _rms(x, g_ref[...])
        return _per_batch(h, 1.0 + mod_ref[:, d:2 * d], mod_ref[:, 0:d]).astype(BF16)

    hb = normed(x_ref[...])
    hb_halo = normed(jnp.concatenate([xp_ref[...], xn_ref[...]], axis=0))

    def mm(r0, c0):
        return jnp.dot(hb[r0:r0 + MIX_DOT_ROWS], w_ref[:, c0:c0 + MIX_COLS], preferred_element_type=F32)

    def xr_task(c0, r0):
        buf[NB + r0:NB + r0 + MIX_DOT_ROWS, c0:c0 + MIX_COLS] = mm(r0, o_xr + c0)

    def xr_halo_task(c0):
        xh = jnp.dot(hb_halo, w_ref[:, o_xr + c0:o_xr + c0 + MIX_COLS], preferred_element_type=F32)
        buf[0:NB, c0:c0 + MIX_COLS] = jnp.where(i > 0, xh[LRU_HALO_ROWS - NB:LRU_HALO_ROWS], 0.0)
        buf[NB + r:, c0:c0 + MIX_COLS] = jnp.where(i < n - 1, xh[LRU_HALO_ROWS:], 0.0)

    def conv_task(c0):
        xc = cb_ref[:, c0:c0 + MIX_COLS] + cw_ref[0:1, c0:c0 + MIX_COLS] * buf[0:r, c0:c0 + MIX_COLS]
        for k in range(1, k_taps):
            xc = xc + cw_ref[k:k + 1, c0:c0 + MIX_COLS] * buf[k * NB:k * NB + r, c0:c0 + MIX_COLS]
        xc_s[:, c0:c0 + MIX_COLS] = xc

    def act_task(c0, r0):
        act_ref[r0:r0 + MIX_DOT_ROWS, c0:c0 + MIX_COLS] = (
            mm(r0, c0) * _sigmoid(mm(r0, cw + c0))).astype(act_ref.dtype)

    def out_task(ref, fn, c_in, c_out, r0):
        ref[r0:r0 + MIX_DOT_ROWS, c_out:c_out + MIX_COLS] = fn(mm(r0, c_in)).astype(ref.dtype)

    row_starts = range(0, r, MIX_DOT_ROWS)
    xr_tasks = ([functools.partial(xr_task, c, r0) for c in range(0, lw, MIX_COLS) for r0 in row_starts]
                + [functools.partial(xr_halo_task, c) for c in range(0, lw, MIX_COLS)])
    act_tasks = [functools.partial(act_task, c, r0) for c in range(0, cw, MIX_COLS) for r0 in row_starts]
    rest = ([functools.partial(out_task, ggt_ref, jax.nn.gelu, o_gt + c, c, r0)
             for c in range(0, lw, MIX_COLS) for r0 in row_starts]
            + [functools.partial(out_task, smg_ref, _sigmoid, o_mg + c, c, r0)
               for c in range(0, 2 * d, MIX_COLS) for r0 in row_starts])
    coeffs = (_lru_coeff_tasks(xc_s, wgf_ref, brf_ref, bif_ref, lamf_ref, af_ref, uf_ref)
              + _lru_coeff_tasks(xc_s, wgb_ref, brb_ref, bib_ref, lamb_ref, ab_s, ub_s))
    convs = [functools.partial(conv_task, c) for c in range(0, lw, MIX_COLS)]
    n_head = len(act_tasks) // 4
    _interleave([(xr_tasks, []), (act_tasks[:n_head], convs), (act_tasks[n_head:] + rest, coeffs)])

    nt = r // NB

    def step(s, h):
        r0 = pl.multiple_of((nt - 1 - s) * NB, NB)
        h = ab_s[pl.ds(r0, NB), :] * h + ub_s[pl.ds(r0, NB), :]
        hb_ref[pl.ds(r0, NB), :] = h
        return h

    h = lax.fori_loop(0, nt, step, h_s[...], unroll=8)
    h_s[...] = h
    st_ref[...] = h


def _mixer_front(x, mod, g_pre, w_in, conv_w, conv_b, gates_f, gates_b, h0_b, cw, lw, tr):
    rows, d = x.shape
    n = rows // tr
    hb = tr // LRU_HALO_ROWS
    last = rows // LRU_HALO_ROWS - 1
    assert tr % MIX_DOT_ROWS == 0 and cw % MIX_COLS == 0 and lw % MIX_COLS == 0 and d % MIX_COLS == 0
    kern = functools.partial(_mixer_front_kernel, cw=cw, lw=lw)
    row = lambda s: (n - 1 - s, 0)
    consts = [mod, g_pre, w_in, conv_w, conv_b, *gates_f, *gates_b, h0_b]
    f32_out = jax.ShapeDtypeStruct((rows, lw), F32)
    return pl.pallas_call(
        kern,
        grid=(n,),
        in_specs=[pl.BlockSpec((LRU_HALO_ROWS, d), lambda s: (jnp.maximum((n - 1 - s) * hb - 1, 0), 0)),
                  pl.BlockSpec((tr, d), row),
                  pl.BlockSpec((LRU_HALO_ROWS, d), lambda s: (jnp.minimum((n - s) * hb, last), 0))]
                 + [_const_spec(a.shape) for a in consts],
        out_specs=[pl.BlockSpec((tr, cw), row), pl.BlockSpec((tr, lw), row), pl.BlockSpec((tr, 2 * d), row)]
                  + [pl.BlockSpec((tr, lw), row)] * 3 + [pl.BlockSpec((NB, lw), lambda s: (0, 0))],
        out_shape=[jax.ShapeDtypeStruct((rows, cw), BF16), jax.ShapeDtypeStruct((rows, lw), BF16),
                   jax.ShapeDtypeStruct((rows, 2 * d), BF16), f32_out, f32_out, f32_out,
                   jax.ShapeDtypeStruct((NB, lw), F32)],
        scratch_shapes=[pltpu.VMEM((tr + NB + LRU_HALO_ROWS, lw), F32), pltpu.VMEM((tr, lw), F32),
                        pltpu.VMEM((tr, lw), F32), pltpu.VMEM((tr, lw), F32), pltpu.VMEM((NB, lw), F32)],
        compiler_params=_params(("arbitrary",)),
        name="mixer_front",
    )(x, x, x, *consts)


def _lru_coeff_tasks(xc, wg_ref, br_ref, bi_ref, lam_ref, a_s, u_s):
    def block(blk, r0):
        lo, hi = blk * GATE_BLK, (blk + 1) * GATE_BLK
        x = xc[r0:r0 + MIX_DOT_ROWS, lo:hi]
        g = jnp.dot(x.astype(BF16), wg_ref[blk], preferred_element_type=F32)
        t_r = jnp.tanh(g[:, :GATE_BLK] + 0.5 * br_ref[:, lo:hi])
        t_i = jnp.tanh(g[:, GATE_BLK:] + 0.5 * bi_ref[:, lo:hi])
        hc = (-0.5 * LRU_C) * _softplus(-lam_ref[:, lo:hi])
        log_a = hc * t_r + hc
        a_s[r0:r0 + MIX_DOT_ROWS, lo:hi] = jnp.exp(log_a)
        th = jnp.tanh(log_a)
        u_s[r0:r0 + MIX_DOT_ROWS, lo:hi] = jnp.sqrt(-0.5 * th) * lax.rsqrt(1.0 - th) * (t_i + 1.0) * x

    return [functools.partial(block, blk, r0) for blk in range(wg_ref.shape[0])
            for r0 in range(0, xc.shape[0], MIX_DOT_ROWS)]


def _dwconv_time(src, dst, w_ref, rows, k_taps):
    lanes = dst.shape[1]
    span = TM_CHUNK + k_taps - 1
    for cb in range(lanes // 128):
        lo, hi = cb * 128, (cb + 1) * 128
        wk = [jnp.broadcast_to(w_ref[k:k + 1, lo:hi], (NB, 128)) for k in range(k_taps)]

        def chunk(c, carry, lo=lo, hi=hi, wk=wk):
            base = pl.multiple_of(c * (TM_CHUNK * NB), TM_CHUNK * NB)
            ins = [src[pl.ds(base + q * NB, NB), lo:hi] for q in range(span)]
            outs = []
            for m in range(TM_CHUNK):
                acc = wk[0] * ins[m]
                for k in range(1, k_taps):
                    acc = acc + wk[k] * ins[m + k]
                outs.append(acc)
            dst[pl.ds(base, TM_CHUNK * NB), lo:hi] = jnp.concatenate(outs, axis=0)
            return carry

        lax.fori_loop(0, rows // (TM_CHUNK * NB), chunk, 0, unroll=2)


def _mixer_back_kernel(ap_ref, am_ref, an_ref, af_ref, uf_ref, hb_ref, ggt_ref, smg_ref, x_ref, mod_ref,
                       dw_ref, lng_ref, lnb_ref, wpc_ref, wpl_ref, wo_ref, gpost_ref, h0_ref,
                       x1_ref, st_ref, cbuf, cout, yb_s, y_s, h_s, lhs_s, ya_s, m_s):
    i = pl.program_id(0)
    n = pl.num_programs(0)
    r, d = x_ref.shape
    k_taps = dw_ref.shape[0]
    halo = (k_taps // 2) * NB

    @pl.when(i == 0)
    def _():
        h_s[...] = h0_ref[...]

    prev = ap_ref[...].astype(F32)
    nxt = an_ref[...].astype(F32)
    cbuf[0:halo, :] = jnp.where(i > 0, prev[CONV_HALO_ROWS - halo:, :], 0.0)
    cbuf[halo:halo + r, :] = am_ref[...].astype(F32)
    cbuf[halo + r:, :] = jnp.where(i < n - 1, nxt[0:halo, :], 0.0)
    _dwconv_time(cbuf, cout, dw_ref, r, k_taps)

    parts = [(p * (r // MIX_PARTS), r // MIX_PARTS) for p in range(MIX_PARTS)]
    col_blocks = range(0, d, MIX_COLS)

    def matmul_tasks(lhs, w_ref, out, r0, rn):
        def piece(c0):
            out[r0:r0 + rn, c0:c0 + MIX_COLS] = jnp.dot(lhs[r0:r0 + rn, :], w_ref[:, c0:c0 + MIX_COLS],
                                                        preferred_element_type=F32)
        return [functools.partial(piece, c0) for c0 in col_blocks]

    def ln_task(r0, rn):
        u = cout[r0:r0 + rn, :]
        mu = jnp.mean(u, axis=-1, keepdims=True)
        uc = u - mu
        var = jnp.mean(uc * uc, axis=-1, keepdims=True)
        ln = uc * lax.rsqrt(var + EPS) * lng_ref[...] + lnb_ref[...]
        lhs_s[r0:r0 + rn, :] = _silu(ln).astype(BF16)

    stages = [([], [functools.partial(ln_task, *parts[0])])]
    for p, part in enumerate(parts):
        vpu = [functools.partial(ln_task, *parts[p + 1])] if p + 1 < len(parts) else []
        stages.append((matmul_tasks(lhs_s, wpc_ref, ya_s, *part), vpu))
    _interleave(stages)

    nt = r // NB

    def step(s, h):
        r0 = pl.multiple_of(s * NB, NB)
        h = af_ref[pl.ds(r0, NB), :] * h + uf_ref[pl.ds(r0, NB), :]
        cout[pl.ds(r0, NB), :] = h + hb_ref[pl.ds(r0, NB), :]
        return h

    h = lax.fori_loop(0, nt, step, h_s[...], unroll=8)
    h_s[...] = h
    st_ref[...] = h

    def gate_task(r0, rn):
        lhs_s[r0:r0 + rn, :] = (cout[r0:r0 + rn, :] * ggt_ref[r0:r0 + rn, :].astype(F32)).astype(BF16)

    def merge_task(r0, rn):
        m = (smg_ref[r0:r0 + rn, 0:d].astype(F32) * ya_s[r0:r0 + rn, :]
             + smg_ref[r0:r0 + rn, d:2 * d].astype(F32) * yb_s[r0:r0 + rn, :])
        m_s[r0:r0 + rn, :] = m.astype(BF16)

    def out_task(r0, rn):
        y = _rms(y_s[r0:r0 + rn, :], gpost_ref[...])
        x1_ref[r0:r0 + rn, :] = x_ref[r0:r0 + rn, :] + _per_batch(y, mod_ref[:, 2 * d:3 * d], None)

    np_ = len(parts)
    stages = []
    for s in range(np_ + 4):
        mxu, vpu = [], []
        if s < np_:
            vpu.append(functools.partial(gate_task, *parts[s]))
        if 1 <= s <= np_:
            mxu += matmul_tasks(lhs_s, wpl_ref, yb_s, *parts[s - 1])
        if 2 <= s <= np_ + 1:
            vpu.append(functools.partial(merge_task, *parts[s - 2]))
        if 3 <= s <= np_ + 2:
            mxu += matmul_tasks(m_s, wo_ref, y_s, *parts[s - 3])
        if 4 <= s <= np_ + 3:
            vpu.append(functools.partial(out_task, *parts[s - 4]))
        stages.append((mxu, vpu))
    _interleave(stages)


def _mixer_back(act, a_f, u_f, hb, ggt, smg, x, mod, dw, lng, lnb, wpc, wpl, wo, gpost, h0, tr):
    rows, d = x.shape
    cw = act.shape[1]
    w = a_f.shape[1]
    assert cw == w == d, "conv / LRU / model widths share scratch buffers"
    assert tr % (MIX_PARTS * 16) == 0 and d % MIX_COLS == 0
    n = rows // tr
    hb_blocks = tr // CONV_HALO_ROWS
    last = rows // CONV_HALO_ROWS - 1
    k_taps = dw.shape[0]
    halo = (k_taps // 2) * NB
    assert halo <= CONV_HALO_ROWS
    row = lambda i: (i, 0)
    consts = [mod, dw, lng, lnb, wpc, wpl, wo, gpost, h0]
    return pl.pallas_call(
        _mixer_back_kernel,
        grid=(n,),
        in_specs=[pl.BlockSpec((CONV_HALO_ROWS, cw), lambda i: (jnp.maximum(i * hb_blocks - 1, 0), 0)),
                  pl.BlockSpec((tr, cw), row),
                  pl.BlockSpec((CONV_HALO_ROWS, cw), lambda i: (jnp.minimum((i + 1) * hb_blocks, last), 0)),
                  pl.BlockSpec((tr, w), row), pl.BlockSpec((tr, w), row), pl.BlockSpec((tr, w), row),
                  pl.BlockSpec((tr, w), row), pl.BlockSpec((tr, 2 * d), row), pl.BlockSpec((tr, d), row)]
                 + [_const_spec(a.shape) for a in consts],
        out_specs=[pl.BlockSpec((tr, d), row), pl.BlockSpec((NB, w), lambda i: (0, 0))],
        out_shape=[jax.ShapeDtypeStruct((rows, d), F32), jax.ShapeDtypeStruct((NB, w), F32)],
        scratch_shapes=[pltpu.VMEM((tr + 2 * halo, cw), F32), pltpu.VMEM((tr, cw), F32),
                        pltpu.VMEM((tr, w), F32), pltpu.VMEM((tr, w), F32), pltpu.VMEM((NB, w), F32),
                        pltpu.VMEM((tr, w), BF16), pltpu.VMEM((tr, d), F32), pltpu.VMEM((tr, d), BF16)],
        compiler_params=_params(("arbitrary",)),
        name="mixer_back",
    )(act, act, act, a_f, u_f, hb, ggt, smg, x, *consts)


def _ffn_up_conv_kernel(x_ref, mod_ref, g_ref, wv_ref, wg_ref, dwv_ref, dwg_ref, hh_ref, zr, *, vert, nt):
    i = pl.program_id(1)
    r, d = x_ref.shape
    f = hh_ref.shape[1]
    k = 3
    nchunk = r // (TM_CHUNK * NB)
    slot_c = 0
    slot_b = 2
    slot_a = 1
    has_before = i > 1
    has_after = i < nt

    @pl.when(i == 0)
    def _():
        zr[1] = jnp.zeros(zr.shape[1:], F32)
        zr[2] = jnp.zeros(zr.shape[1:], F32)

    def conv_tasks(lo):
        arrays = []
        for dw_ref, col in ((dwv_ref, lo), (dwg_ref, f + lo)):
            w = {(dy, dx): jnp.broadcast_to(dw_ref[dy * k + dx:dy * k + dx + 1, lo:lo + 128], (NB, 128))
                 for dy in range(k) for dx in range(k)}
            if vert:
                srcs = ((slot_a, {dx: jnp.where(has_before, w[(0, dx)], 0.0) for dx in range(k)}),
                        (slot_b, {dx: w[(1, dx)] for dx in range(k)}),
                        (slot_c, {dx: jnp.where(has_after, w[(2, dx)], 0.0) for dx in range(k)}))
                edge = None
            else:
                srcs = ((slot_b, {dx: w[(1, dx)] for dx in range(k)}),)
                edge = (jnp.where(has_before, w[(1, 0)], 0.0), jnp.where(has_after, w[(1, 2)], 0.0))
            arrays.append((col, srcs, edge))
        return [functools.partial(conv_chunk, lo, arrays, q) for q in range(nchunk)]

    def conv_chunk(lo, arrays, q):
        base = q * TM_CHUNK * NB
        lo_r = max(base - NB, 0)
        hi_r = min(base + (TM_CHUNK + 1) * NB, r)
        res = []
        for col, srcs, edge in arrays:
            pc, ln = col // FFN_COLS, col % FFN_COLS
            blks = [zr[slot, pc, lo_r:hi_r, ln:ln + 128] for slot, _ in srcs]
            outs = []
            for m in range(TM_CHUNK):
                acc = None
                for (_, wt), blk in zip(srcs, blks):
                    for dx in range(k):
                        row = base + (m + dx - 1) * NB
                        if row < 0 or row >= r:
                            continue
                        term = wt[dx] * blk[row - lo_r:row - lo_r + NB]
                        acc = term if acc is None else acc + term
                if edge is not None and q == 0 and m == 0:
                    acc = acc + edge[0] * zr[slot_a, pc, r - NB:r, ln:ln + 128]
                if edge is not None and q == nchunk - 1 and m == TM_CHUNK - 1:
                    acc = acc + edge[1] * zr[slot_c, pc, 0:NB, ln:ln + 128]
                outs.append(acc)
            res.append(jnp.concatenate(outs, axis=0))
        hh_ref[base:base + TM_CHUNK * NB, lo:lo + 128] = (_silu(res[1]) * res[0]).astype(hh_ref.dtype)

    h = _rms(x_ref[...], g_ref[...])
    h = _per_batch(h, 1.0 + mod_ref[:, 4 * d:5 * d], mod_ref[:, 3 * d:4 * d])
    hb = h.astype(BF16)
    nk = f // FFN_COLS

    def dot_task(w_ref, off, c0, r0):
        zr[slot_c, (off + c0) // FFN_COLS, r0:r0 + FFN_DOT_ROWS] = jnp.dot(
            hb[r0:r0 + FFN_DOT_ROWS], w_ref[:, c0:c0 + FFN_COLS], preferred_element_type=F32)

    def age_task(pc):
        zr[slot_a, pc] = zr[slot_b, pc]
        zr[slot_b, pc] = zr[slot_c, pc]

    stages = []
    for kk in range(nk + 1):
        mxu_tasks, vpu_tasks = [], []
        if kk < nk:
            c0 = kk * FFN_COLS
            mxu_tasks = [functools.partial(dot_task, w_ref, off, c0, r0)
                         for w_ref, off in ((wv_ref, 0), (wg_ref, f)) for r0 in range(0, r, FFN_DOT_ROWS)]
        if kk >= 1:
            c0 = (kk - 1) * FFN_COLS
            for cb in range(FFN_COLS // 128):
                vpu_tasks += conv_tasks(c0 + cb * 128)
            vpu_tasks += [functools.partial(age_task, (off + c0) // FFN_COLS) for off in (0, f)]
        stages.append((mxu_tasks, vpu_tasks))
    _interleave(stages)


def _ffn_up_conv(x, mod, g_pre, w_up, dwf, vert):
    rows, d = x.shape
    f = w_up.shape[1] // 2
    tr = FFN_TILE_STEPS * NB
    nt = rows // tr
    fb = f // FFN_CHANNEL_BLOCKS
    nb = FFN_CHANNEL_BLOCKS
    assert rows % tr == 0 and fb % FFN_COLS == 0 and (not vert or FFN_TILE_STEPS == GRID_W)
    return pl.pallas_call(
        functools.partial(_ffn_up_conv_kernel, vert=vert, nt=nt),
        grid=(nb, nt + 1),
        in_specs=[pl.BlockSpec((tr, d), lambda p, i: (jnp.minimum(i, nt - 1), 0)), _const_spec(mod.shape),
                  _const_spec(g_pre.shape),
                  pl.BlockSpec((d, fb), lambda p, i: (0, p)), pl.BlockSpec((d, fb), lambda p, i: (0, nb + p)),
                  pl.BlockSpec((dwf.shape[0], fb), lambda p, i: (0, p)),
                  pl.BlockSpec((dwf.shape[0], fb), lambda p, i: (0, nb + p))],
        out_specs=pl.BlockSpec((tr, fb), lambda p, i: (jnp.maximum(i - 1, 0), p)),
        out_shape=jax.ShapeDtypeStruct((rows, f), BF16),
        scratch_shapes=[pltpu.VMEM((3, 2 * fb // FFN_COLS, tr, FFN_COLS), F32)],
        compiler_params=_params(("arbitrary", "arbitrary")),
        name="ffn_up_conv",
    )(x, mod, g_pre, w_up, w_up, dwf, dwf)


def _ffn_down_kernel(hh_ref, wd_ref, x_ref, mod_ref, gpost_ref, o_ref):
    d = x_ref.shape[1]
    y = jnp.dot(hh_ref[...], wd_ref[...], preferred_element_type=F32)
    o_ref[...] = x_ref[...] + _per_batch(_rms(y, gpost_ref[...]), mod_ref[:, 5 * d:6 * d], None)


def _ffn_down(hh, w_down, x, mod, gpost, tr):
    rows, d = x.shape
    f = hh.shape[1]
    row = lambda i: (i, 0)
    return pl.pallas_call(
        _ffn_down_kernel,
        grid=(rows // tr,),
        in_specs=[pl.BlockSpec((tr, f), row), _const_spec(w_down.shape), pl.BlockSpec((tr, d), row),
                  _const_spec(mod.shape), _const_spec(gpost.shape)],
        out_specs=pl.BlockSpec((tr, d), row),
        out_shape=jax.ShapeDtypeStruct((rows, d), F32),
        compiler_params=_params(("arbitrary",)),
        name="ffn_down",
    )(hh, w_down, x, mod, gpost)


def _gate_blocks(w_r, w_i):
    h, hd, _ = w_r.shape
    hpb = GATE_BLK // hd
    nblk = h // hpb
    eye = jnp.eye(hpb, dtype=w_r.dtype)

    def bd(w):
        w = w.reshape(nblk, hpb, hd, hd)
        return (w[:, :, :, None, :] * eye[None, :, None, :, None]).reshape(nblk, GATE_BLK, GATE_BLK)

    return (0.5 * jnp.concatenate([bd(w_r), bd(w_i)], axis=-1)).astype(BF16)


def _row_tile(rows, want):
    t = min(want, rows)
    while rows % t:
        t //= 2
    return t


def kernel(x, c, ctx, c_ctx, w_ada, b_ada, g_pre_mix, g_post_mix, g_pre_ffn, g_post_ffn, w_in, dw_conv,
           ln_conv_g, ln_conv_b, w_proj_conv, lru_conv_w, lru_conv_b, w_rgate, b_rgate, w_igate, b_igate,
           lru_lambda, w_proj_lru, w_out, w_up, dw_ffn, w_down):
    b, t_lat, d = x.shape
    t_ctx = ctx.shape[1]
    depth = w_in.shape[0]
    cw = dw_conv.shape[-1]
    lw = lru_conv_w.shape[-1]
    f = w_down.shape[1]
    assert b == NB and t_lat % GRID_W == 0

    xl = jnp.transpose(x, (1, 0, 2)).reshape(t_lat * NB, d)
    xc_ = jnp.transpose(ctx, (1, 0, 2)).reshape(t_ctx * NB, d)

    cc = jnp.zeros((2 * NB, d), F32).at[0:NB].set(c).at[NB].set(c_ctx)
    mod_all = _ada_call(cc, w_ada, b_ada)

    row2 = lambda a: a.reshape(1, -1)
    zeros_state = jnp.zeros((NB, lw), F32)

    for l in range(depth):
        need_ctx = l < depth - 1
        mod_lat = mod_all[l, 0:NB]
        mod_ctx = jnp.broadcast_to(mod_all[l, NB:NB + 1], (NB, mod_all.shape[-1]))
        w_in_l = w_in[l].astype(BF16)
        wg = [_gate_blocks(w_rgate[l, dd], w_igate[l, dd]) for dd in range(2)]
        br = [row2(b_rgate[l, dd]) for dd in range(2)]
        bi = [row2(b_igate[l, dd]) for dd in range(2)]
        lam = [row2(lru_lambda[l, dd]) for dd in range(2)]
        wpc = w_proj_conv[l].astype(BF16)
        wpl = w_proj_lru[l].astype(BF16)
        wo = w_out[l].astype(BF16)
        wup = w_up[l].astype(BF16)
        wdn = w_down[l].astype(BF16)
        dwf = dw_ffn[l].reshape(-1, 2 * f)

        def mixer(xrows, mod, h0_b, h0_f):
            rows = xrows.shape[0]
            tr = _row_tile(rows, MIX_TILE_ROWS)
            act, ggt, smg, a_f, u_f, hbwd, st_b = _mixer_front(
                xrows, mod, row2(g_pre_mix[l]), w_in_l, lru_conv_w[l], row2(lru_conv_b[l]),
                (wg[0], br[0], bi[0], lam[0]), (wg[1], br[1], bi[1], lam[1]), h0_b, cw, lw, tr)
            x1, st_f = _mixer_back(act, a_f, u_f, hbwd, ggt, smg, xrows, mod, dw_conv[l], row2(ln_conv_g[l]),
                                   row2(ln_conv_b[l]), wpc, wpl, wo, row2(g_post_mix[l]), h0_f, tr)
            return x1, st_b, st_f

        def ffn(x1, mod, vert):
            hh = _ffn_up_conv(x1, mod, row2(g_pre_ffn[l]), wup, dwf, vert)
            return _ffn_down(hh, wdn, x1, mod, row2(g_post_ffn[l]), _row_tile(x1.shape[0], MIX_TILE_ROWS))

        c1, st_b, st_f = mixer(xc_, mod_ctx, zeros_state, zeros_state)
        x1, _, _ = mixer(xl, mod_lat, st_b, st_f)
        xl = ffn(x1, mod_lat, True)
        if need_ctx:
            xc_ = ffn(c1, mod_ctx, False)

    return jnp.transpose(xl.reshape(t_lat, NB, d), (1, 0, 2))
```

```python
import functools

import jax
import jax.numpy as jnp
from jax import lax
from jax.experimental import pallas as pl
from jax.experimental.pallas import tpu as pltpu

F32 = jnp.float32
BF16 = jnp.bfloat16

NB = 8
EPS = 1e-6
GRID_W = 64
LRU_C = 8.0
GATE_BLK = 256
V7X_VMEM_LIMIT = 56 * 1024 * 1024
CONV_HALO_ROWS = 128
LRU_HALO_ROWS = 16
FFN_TILE_STEPS = GRID_W
FFN_COLS = 256
FFN_DOT_ROWS = 128
FFN_CHANNEL_BLOCKS = 2
TM_CHUNK = 8
MIX_TILE_ROWS = 512
MIX_PARTS = 2
MIX_COLS = 256
MIX_DOT_ROWS = 256
LRU_TASK_ROWS = 256


def _const_spec(shape):
    nd = len(shape)
    return pl.BlockSpec(shape, lambda *_: (0,) * nd, pipeline_mode=pl.Buffered(1))


def _params(sem):
    return pltpu.CompilerParams(dimension_semantics=sem, vmem_limit_bytes=V7X_VMEM_LIMIT)


def _rms(x, g):
    ms = jnp.mean(x * x, axis=-1, keepdims=True)
    return x * lax.rsqrt(ms + EPS) * g


def _per_batch(x, mul, add):
    r, d = x.shape
    y = x.reshape(r // NB, NB, d) * mul[None]
    if add is not None:
        y = y + add[None]
    return y.reshape(r, d)


def _sigmoid(x):
    return 0.5 * jnp.tanh(0.5 * x) + 0.5


def _silu(x):
    return x * _sigmoid(x)


def _softplus(z):
    return jnp.maximum(z, 0.0) + jnp.log1p(jnp.exp(-jnp.abs(z)))


def _interleave(stages):
    for mxu_tasks, vpu_tasks in stages:
        nm, nv = len(mxu_tasks), len(vpu_tasks)
        done = 0
        for k, task in enumerate(mxu_tasks):
            task()
            want = -(-(k + 1) * nv // nm)
            for v in vpu_tasks[done:want]:
                v()
            done = max(done, want)
        for v in vpu_tasks[done:]:
            v()


def _ada_kernel(c_ref, w_ref, b_ref, o_ref):
    s = jax.nn.silu(c_ref[...])
    o_ref[0] = jnp.dot(s, w_ref[0], preferred_element_type=F32) + b_ref[0]


def _ada_call(cc, w_ada, b_ada):
    depth, d, n = w_ada.shape
    tn = 1024 if n % 1024 == 0 else n
    return pl.pallas_call(
        _ada_kernel,
        grid=(depth, n // tn),
        in_specs=[pl.BlockSpec(cc.shape, lambda l, j: (0, 0)),
                  pl.BlockSpec((1, d, tn), lambda l, j: (l, 0, j)),
                  pl.BlockSpec((1, 1, tn), lambda l, j: (l, 0, j))],
        out_specs=pl.BlockSpec((1, cc.shape[0], tn), lambda l, j: (l, 0, j)),
        out_shape=jax.ShapeDtypeStruct((depth, cc.shape[0], n), F32),
        compiler_params=_params(("arbitrary", "arbitrary")),
        name="ada_mod",
    )(cc, w_ada, b_ada.reshape(depth, 1, n))


def _mixer_front_kernel(xp_ref, x_ref, xn_ref, mod_ref, g_ref, w_ref, cw_ref, cb_ref,
                        wgf_ref, brf_ref, bif_ref, lamf_ref, wgb_ref, brb_ref, bib_ref, lamb_ref, h0_ref,
                        *refs, cw, lw, state_only):
    if state_only:
        af_ref, uf_ref, hb_ref, st_ref, buf, xc_s, ab_s, ub_s, h_s = refs
    else:
        act_ref, ggt_ref, smg_ref, af_ref, uf_ref, hb_ref, st_ref, buf, xc_s, ab_s, ub_s, h_s = refs
    step_id = pl.program_id(0)
    n = pl.num_programs(0)
    i = n - 1 - step_id
    r, d = x_ref.shape

    @pl.when(step_id == 0)
    def _():
        h_s[...] = h0_ref[...]

    k_taps = cw_ref.shape[0]
    o_xr, o_gt, o_mg = 2 * cw, 2 * cw + lw, 2 * cw + 2 * lw

    def normed(x):
        h = _rms(x, g_ref[...])
        return _per_batch(h, 1.0 + mod_ref[:, d:2 * d], mod_ref[:, 0:d]).astype(BF16)

    hb = normed(x_ref[...])
    hb_halo = normed(jnp.concatenate([xp_ref[...], xn_ref[...]], axis=0))

    def mm(r0, c0):
        return jnp.dot(hb[r0:r0 + MIX_DOT_ROWS], w_ref[:, c0:c0 + MIX_COLS], preferred_element_type=F32)

    def xr_task(c0, r0):
        buf[NB + r0:NB + r0 + MIX_DOT_ROWS, c0:c0 + MIX_COLS] = mm(r0, o_xr + c0)

    def xr_halo_task(c0):
        xh = jnp.dot(hb_halo, w_ref[:, o_xr + c0:o_xr + c0 + MIX_COLS], preferred_element_type=F32)
        buf[0:NB, c0:c0 + MIX_COLS] = jnp.where(i > 0, xh[LRU_HALO_ROWS - NB:LRU_HALO_ROWS], 0.0)
        buf[NB + r:, c0:c0 + MIX_COLS] = jnp.where(i < n - 1, xh[LRU_HALO_ROWS:], 0.0)

    def conv_task(c0):
        xc = cb_ref[:, c0:c0 + MIX_COLS] + cw_ref[0:1, c0:c0 + MIX_COLS] * buf[0:r, c0:c0 + MIX_COLS]
        for k in range(1, k_taps):
            xc = xc + cw_ref[k:k + 1, c0:c0 + MIX_COLS] * buf[k * NB:k * NB + r, c0:c0 + MIX_COLS]
        xc_s[:, c0:c0 + MIX_COLS] = xc

    def act_task(c0, r0):
        act_ref[r0:r0 + MIX_DOT_ROWS, c0:c0 + MIX_COLS] = (
            mm(r0, c0) * _sigmoid(mm(r0, cw + c0))).astype(act_ref.dtype)

    def out_task(ref, fn, c_in, c_out, r0):
        ref[r0:r0 + MIX_DOT_ROWS, c_out:c_out + MIX_COLS] = fn(mm(r0, c_in)).astype(ref.dtype)

    row_starts = range(0, r, MIX_DOT_ROWS)
    xr_tasks = ([functools.partial(xr_task, c, r0) for c in range(0, lw, MIX_COLS) for r0 in row_starts]
                + [functools.partial(xr_halo_task, c) for c in range(0, lw, MIX_COLS)])
    coeffs = (_lru_coeff_tasks(xc_s, wgf_ref, brf_ref, bif_ref, lamf_ref, af_ref, uf_ref)
              + _lru_coeff_tasks(xc_s, wgb_ref, brb_ref, bib_ref, lamb_ref, ab_s, ub_s))
    convs = [functools.partial(conv_task, c) for c in range(0, lw, MIX_COLS)]
    if state_only:
        _interleave([(xr_tasks, []), ([], convs + coeffs)])
    else:
        act_tasks = [functools.partial(act_task, c, r0) for c in range(0, cw, MIX_COLS) for r0 in row_starts]
        rest = ([functools.partial(out_task, ggt_ref, jax.nn.gelu, o_gt + c, c, r0)
                 for c in range(0, lw, MIX_COLS) for r0 in row_starts]
                + [functools.partial(out_task, smg_ref, _sigmoid, o_mg + c, c, r0)
                   for c in range(0, 2 * d, MIX_COLS) for r0 in row_starts])
        n_head = len(act_tasks) // 4
        _interleave([(xr_tasks, []), (act_tasks[:n_head], convs), (act_tasks[n_head:] + rest, coeffs)])

    nt = r // NB

    def step(s, h):
        r0 = pl.multiple_of((nt - 1 - s) * NB, NB)
        h = ab_s[pl.ds(r0, NB), :] * h + ub_s[pl.ds(r0, NB), :]
        hb_ref[pl.ds(r0, NB), :] = h
        return h

    h = lax.fori_loop(0, nt, step, h_s[...], unroll=8)
    h_s[...] = h
    st_ref[...] = h


def _mixer_front(x, mod, g_pre, w_in, conv_w, conv_b, gates_f, gates_b, h0_b, cw, lw, tr, state_only=False):
    rows, d = x.shape
    n = rows // tr
    hb = tr // LRU_HALO_ROWS
    last = rows // LRU_HALO_ROWS - 1
    assert tr % MIX_DOT_ROWS == 0 and cw % MIX_COLS == 0 and lw % MIX_COLS == 0 and d % MIX_COLS == 0
    kern = functools.partial(_mixer_front_kernel, cw=cw, lw=lw, state_only=state_only)
    row = lambda s: (n - 1 - s, 0)
    consts = [mod, g_pre, w_in, conv_w, conv_b, *gates_f, *gates_b, h0_b]
    f32_out = jax.ShapeDtypeStruct((rows, lw), F32)
    main_specs = [pl.BlockSpec((tr, cw), row), pl.BlockSpec((tr, lw), row), pl.BlockSpec((tr, 2 * d), row)]
    main_shapes = [jax.ShapeDtypeStruct((rows, cw), BF16), jax.ShapeDtypeStruct((rows, lw), BF16),
                   jax.ShapeDtypeStruct((rows, 2 * d), BF16)]
    return pl.pallas_call(
        kern,
        grid=(n,),
        in_specs=[pl.BlockSpec((LRU_HALO_ROWS, d), lambda s: (jnp.maximum((n - 1 - s) * hb - 1, 0), 0)),
                  pl.BlockSpec((tr, d), row),
                  pl.BlockSpec((LRU_HALO_ROWS, d), lambda s: (jnp.minimum((n - s) * hb, last), 0))]
                 + [_const_spec(a.shape) for a in consts],
        out_specs=([] if state_only else main_specs)
                  + [pl.BlockSpec((tr, lw), row)] * 3 + [pl.BlockSpec((NB, lw), lambda s: (0, 0))],
        out_shape=([] if state_only else main_shapes)
                  + [f32_out, f32_out, f32_out, jax.ShapeDtypeStruct((NB, lw), F32)],
        scratch_shapes=[pltpu.VMEM((tr + NB + LRU_HALO_ROWS, lw), F32), pltpu.VMEM((tr, lw), F32),
                        pltpu.VMEM((tr, lw), F32), pltpu.VMEM((tr, lw), F32), pltpu.VMEM((NB, lw), F32)],
        compiler_params=_params(("arbitrary",)),
        name="mixer_front",
    )(x, x, x, *consts)


def _lru_coeff_tasks(xc, wg_ref, br_ref, bi_ref, lam_ref, a_s, u_s):
    def block(blk, r0):
        lo, hi = blk * GATE_BLK, (blk + 1) * GATE_BLK
        x = xc[r0:r0 + LRU_TASK_ROWS, lo:hi]
        g = jnp.dot(x.astype(BF16), wg_ref[blk], preferred_element_type=F32)
        t_r = jnp.tanh(g[:, :GATE_BLK] + 0.5 * br_ref[:, lo:hi])
        t_i = jnp.tanh(g[:, GATE_BLK:] + 0.5 * bi_ref[:, lo:hi])
        hc = (-0.5 * LRU_C) * _softplus(-lam_ref[:, lo:hi])
        log_a = hc * t_r + hc
        a_s[r0:r0 + LRU_TASK_ROWS, lo:hi] = jnp.exp(log_a)
        th = jnp.tanh(log_a)
        u_s[r0:r0 + LRU_TASK_ROWS, lo:hi] = jnp.sqrt(-0.5 * th) * lax.rsqrt(1.0 - th) * (t_i + 1.0) * x

    return [functools.partial(block, blk, r0) for blk in range(wg_ref.shape[0])
            for r0 in range(0, xc.shape[0], LRU_TASK_ROWS)]


def _scan_state_kernel(a_ref, u_ref, h0_ref, st_ref, h_s):
    nt = a_ref.shape[0] // NB

    @pl.when(pl.program_id(0) == 0)
    def _():
        h_s[...] = h0_ref[...]

    def step(s, h):
        r0 = pl.multiple_of(s * NB, NB)
        return a_ref[pl.ds(r0, NB), :] * h + u_ref[pl.ds(r0, NB), :]

    h = lax.fori_loop(0, nt, step, h_s[...], unroll=8)
    h_s[...] = h
    st_ref[...] = h


def _scan_state(a, u, h0, tr):
    rows, w = a.shape
    row = lambda i: (i, 0)
    return pl.pallas_call(
        _scan_state_kernel,
        grid=(rows // tr,),
        in_specs=[pl.BlockSpec((tr, w), row), pl.BlockSpec((tr, w), row), _const_spec(h0.shape)],
        out_specs=pl.BlockSpec((NB, w), lambda i: (0, 0)),
        out_shape=jax.ShapeDtypeStruct((NB, w), F32),
        scratch_shapes=[pltpu.VMEM((NB, w), F32)],
        compiler_params=_params(("arbitrary",)),
        name="lru_scan_state",
    )(a, u, h0)


def _dwconv_time(src, dst, w_ref, rows, k_taps):
    lanes = dst.shape[1]
    span = TM_CHUNK + k_taps - 1
    for cb in range(lanes // 128):
        lo, hi = cb * 128, (cb + 1) * 128
        wk = [jnp.broadcast_to(w_ref[k:k + 1, lo:hi], (NB, 128)) for k in range(k_taps)]

        def chunk(c, carry, lo=lo, hi=hi, wk=wk):
            base = pl.multiple_of(c * (TM_CHUNK * NB), TM_CHUNK * NB)
            ins = [src[pl.ds(base + q * NB, NB), lo:hi] for q in range(span)]
            outs = []
            for m in range(TM_CHUNK):
                acc = wk[0] * ins[m]
                for k in range(1, k_taps):
                    acc = acc + wk[k] * ins[m + k]
                outs.append(acc)
            dst[pl.ds(base, TM_CHUNK * NB), lo:hi] = jnp.concatenate(outs, axis=0)
            return carry

        lax.fori_loop(0, rows // (TM_CHUNK * NB), chunk, 0, unroll=4)


def _mixer_back_kernel(ap_ref, am_ref, an_ref, af_ref, uf_ref, hb_ref, ggt_ref, smg_ref, x_ref, mod_ref,
                       dw_ref, lng_ref, lnb_ref, wpc_ref, wpl_ref, wo_ref, gpost_ref, h0_ref,
                       x1_ref, st_ref, cbuf, cout, yb_s, y_s, h_s, lhs_s, ya_s, m_s):
    i = pl.program_id(0)
    n = pl.num_programs(0)
    r, d = x_ref.shape
    k_taps = dw_ref.shape[0]
    halo = (k_taps // 2) * NB

    @pl.when(i == 0)
    def _():
        h_s[...] = h0_ref[...]

    prev = ap_ref[...].astype(F32)
    nxt = an_ref[...].astype(F32)
    cbuf[0:halo, :] = jnp.where(i > 0, prev[CONV_HALO_ROWS - halo:, :], 0.0)
    cbuf[halo:halo + r, :] = am_ref[...].astype(F32)
    cbuf[halo + r:, :] = jnp.where(i < n - 1, nxt[0:halo, :], 0.0)
    _dwconv_time(cbuf, cout, dw_ref, r, k_taps)

    parts = [(p * (r // MIX_PARTS), r // MIX_PARTS) for p in range(MIX_PARTS)]
    col_blocks = range(0, d, MIX_COLS)

    def matmul_tasks(lhs, w_ref, out, r0, rn):
        def piece(c0):
            out[r0:r0 + rn, c0:c0 + MIX_COLS] = jnp.dot(lhs[r0:r0 + rn, :], w_ref[:, c0:c0 + MIX_COLS],
                                                        preferred_element_type=F32)
        return [functools.partial(piece, c0) for c0 in col_blocks]

    def ln_task(r0, rn):
        u = cout[r0:r0 + rn, :]
        mu = jnp.mean(u, axis=-1, keepdims=True)
        uc = u - mu
        var = jnp.mean(uc * uc, axis=-1, keepdims=True)
        ln = uc * lax.rsqrt(var + EPS) * lng_ref[...] + lnb_ref[...]
        lhs_s[r0:r0 + rn, :] = _silu(ln).astype(BF16)

    stages = [([], [functools.partial(ln_task, *parts[0])])]
    for p, part in enumerate(parts):
        vpu = [functools.partial(ln_task, *parts[p + 1])] if p + 1 < len(parts) else []
        stages.append((matmul_tasks(lhs_s, wpc_ref, ya_s, *part), vpu))
    _interleave(stages)

    nt = r // NB

    def step(s, h):
        r0 = pl.multiple_of(s * NB, NB)
        h = af_ref[pl.ds(r0, NB), :] * h + uf_ref[pl.ds(r0, NB), :]
        cout[pl.ds(r0, NB), :] = h + hb_ref[pl.ds(r0, NB), :]
        return h

    h = lax.fori_loop(0, nt, step, h_s[...], unroll=8)
    h_s[...] = h
    st_ref[...] = h

    def gate_task(r0, rn):
        lhs_s[r0:r0 + rn, :] = (cout[r0:r0 + rn, :] * ggt_ref[r0:r0 + rn, :].astype(F32)).astype(BF16)

    def merge_task(r0, rn):
        m = (smg_ref[r0:r0 + rn, 0:d].astype(F32) * ya_s[r0:r0 + rn, :]
             + smg_ref[r0:r0 + rn, d:2 * d].astype(F32) * yb_s[r0:r0 + rn, :])
        m_s[r0:r0 + rn, :] = m.astype(BF16)

    def out_task(r0, rn):
        y = _rms(y_s[r0:r0 + rn, :], gpost_ref[...])
        x1_ref[r0:r0 + rn, :] = x_ref[r0:r0 + rn, :] + _per_batch(y, mod_ref[:, 2 * d:3 * d], None)

    np_ = len(parts)
    stages = []
    for s in range(np_ + 4):
        mxu, vpu = [], []
        if s < np_:
            vpu.append(functools.partial(gate_task, *parts[s]))
        if 1 <= s <= np_:
            mxu += matmul_tasks(lhs_s, wpl_ref, yb_s, *parts[s - 1])
        if 2 <= s <= np_ + 1:
            vpu.append(functools.partial(merge_task, *parts[s - 2]))
        if 3 <= s <= np_ + 2:
            mxu += matmul_tasks(m_s, wo_ref, y_s, *parts[s - 3])
        if 4 <= s <= np_ + 3:
            vpu.append(functools.partial(out_task, *parts[s - 4]))
        stages.append((mxu, vpu))
    _interleave(stages)


def _mixer_back(act, a_f, u_f, hb, ggt, smg, x, mod, dw, lng, lnb, wpc, wpl, wo, gpost, h0, tr):
    rows, d = x.shape
    cw = act.shape[1]
    w = a_f.shape[1]
    assert cw == w == d, "conv / LRU / model widths share scratch buffers"
    assert tr % (MIX_PARTS * 16) == 0 and d % MIX_COLS == 0
    n = rows // tr
    hb_blocks = tr // CONV_HALO_ROWS
    last = rows // CONV_HALO_ROWS - 1
    k_taps = dw.shape[0]
    halo = (k_taps // 2) * NB
    assert halo <= CONV_HALO_ROWS
    row = lambda i: (i, 0)
    consts = [mod, dw, lng, lnb, wpc, wpl, wo, gpost, h0]
    return pl.pallas_call(
        _mixer_back_kernel,
        grid=(n,),
        in_specs=[pl.BlockSpec((CONV_HALO_ROWS, cw), lambda i: (jnp.maximum(i * hb_blocks - 1, 0), 0)),
                  pl.BlockSpec((tr, cw), row),
                  pl.BlockSpec((CONV_HALO_ROWS, cw), lambda i: (jnp.minimum((i + 1) * hb_blocks, last), 0)),
                  pl.BlockSpec((tr, w), row), pl.BlockSpec((tr, w), row), pl.BlockSpec((tr, w), row),
                  pl.BlockSpec((tr, w), row), pl.BlockSpec((tr, 2 * d), row), pl.BlockSpec((tr, d), row)]
                 + [_const_spec(a.shape) for a in consts],
        out_specs=[pl.BlockSpec((tr, d), row), pl.BlockSpec((NB, w), lambda i: (0, 0))],
        out_shape=[jax.ShapeDtypeStruct((rows, d), F32), jax.ShapeDtypeStruct((NB, w), F32)],
        scratch_shapes=[pltpu.VMEM((tr + 2 * halo, cw), F32), pltpu.VMEM((tr, cw), F32),
                        pltpu.VMEM((tr, w), F32), pltpu.VMEM((tr, w), F32), pltpu.VMEM((NB, w), F32),
                        pltpu.VMEM((tr, w), BF16), pltpu.VMEM((tr, d), F32), pltpu.VMEM((tr, d), BF16)],
        compiler_params=_params(("arbitrary",)),
        name="mixer_back",
    )(act, act, act, a_f, u_f, hb, ggt, smg, x, *consts)


def _ffn_up_conv_kernel(x_ref, mod_ref, g_ref, wv_ref, wg_ref, dwv_ref, dwg_ref, hh_ref, zr, *, vert, nt):
    i = pl.program_id(1)
    r, d = x_ref.shape
    f = hh_ref.shape[1]
    k = 3
    nchunk = r // (TM_CHUNK * NB)
    slot_c = 0
    slot_b = 2
    slot_a = 1
    has_before = i > 1
    has_after = i < nt

    @pl.when(i == 0)
    def _():
        zr[1] = jnp.zeros(zr.shape[1:], F32)
        zr[2] = jnp.zeros(zr.shape[1:], F32)

    def conv_tasks(lo):
        arrays = []
        for dw_ref, col in ((dwv_ref, lo), (dwg_ref, f + lo)):
            w = {(dy, dx): jnp.broadcast_to(dw_ref[dy * k + dx:dy * k + dx + 1, lo:lo + 128], (NB, 128))
                 for dy in range(k) for dx in range(k)}
            if vert:
                srcs = ((slot_a, {dx: jnp.where(has_before, w[(0, dx)], 0.0) for dx in range(k)}),
                        (slot_b, {dx: w[(1, dx)] for dx in range(k)}),
                        (slot_c, {dx: jnp.where(has_after, w[(2, dx)], 0.0) for dx in range(k)}))
                edge = None
            else:
                srcs = ((slot_b, {dx: w[(1, dx)] for dx in range(k)}),)
                edge = (jnp.where(has_before, w[(1, 0)], 0.0), jnp.where(has_after, w[(1, 2)], 0.0))
            arrays.append((col, srcs, edge))
        return [functools.partial(conv_chunk, lo, arrays, q) for q in range(nchunk)]

    def conv_chunk(lo, arrays, q):
        base = q * TM_CHUNK * NB
        lo_r = max(base - NB, 0)
        hi_r = min(base + (TM_CHUNK + 1) * NB, r)
        res = []
        for col, srcs, edge in arrays:
            pc, ln = col // FFN_COLS, col % FFN_COLS
            blks = [zr[slot, pc, lo_r:hi_r, ln:ln + 128] for slot, _ in srcs]
            outs = []
            for m in range(TM_CHUNK):
                acc = None
                for (_, wt), blk in zip(srcs, blks):
                    for dx in range(k):
                        row = base + (m + dx - 1) * NB
                        if row < 0 or row >= r:
                            continue
                        term = wt[dx] * blk[row - lo_r:row - lo_r + NB]
                        acc = term if acc is None else acc + term
                if edge is not None and q == 0 and m == 0:
                    acc = acc + edge[0] * zr[slot_a, pc, r - NB:r, ln:ln + 128]
                if edge is not None and q == nchunk - 1 and m == TM_CHUNK - 1:
                    acc = acc + edge[1] * zr[slot_c, pc, 0:NB, ln:ln + 128]
                outs.append(acc)
            res.append(jnp.concatenate(outs, axis=0))
        half_g = 0.5 * res[1]
        hh_ref[base:base + TM_CHUNK * NB, lo:lo + 128] = (
            (half_g * res[0]) * (jnp.tanh(half_g) + 1.0)).astype(hh_ref.dtype)

    h = _rms(x_ref[...], g_ref[...])
    h = _per_batch(h, 1.0 + mod_ref[:, 4 * d:5 * d], mod_ref[:, 3 * d:4 * d])
    hb = h.astype(BF16)
    nk = f // FFN_COLS

    def dot_task(w_ref, off, c0, r0):
        zr[slot_c, (off + c0) // FFN_COLS, r0:r0 + FFN_DOT_ROWS] = jnp.dot(
            hb[r0:r0 + FFN_DOT_ROWS], w_ref[:, c0:c0 + FFN_COLS], preferred_element_type=F32)

    def age_task(pc):
        zr[slot_a, pc] = zr[slot_b, pc]
        zr[slot_b, pc] = zr[slot_c, pc]

    stages = []
    for kk in range(nk + 1):
        mxu_tasks, vpu_tasks = [], []
        if kk < nk:
            c0 = kk * FFN_COLS
            mxu_tasks = [functools.partial(dot_task, w_ref, off, c0, r0)
                         for w_ref, off in ((wv_ref, 0), (wg_ref, f)) for r0 in range(0, r, FFN_DOT_ROWS)]
        if kk >= 1:
            c0 = (kk - 1) * FFN_COLS
            for cb in range(FFN_COLS // 128):
                vpu_tasks += conv_tasks(c0 + cb * 128)
            vpu_tasks += [functools.partial(age_task, (off + c0) // FFN_COLS) for off in (0, f)]
        stages.append((mxu_tasks, vpu_tasks))
    _interleave(stages)


def _ffn_up_conv(x, mod, g_pre, w_up, dwf, vert):
    rows, d = x.shape
    f = w_up.shape[1] // 2
    tr = FFN_TILE_STEPS * NB
    nt = rows // tr
    fb = f // FFN_CHANNEL_BLOCKS
    nb = FFN_CHANNEL_BLOCKS
    assert rows % tr == 0 and fb % FFN_COLS == 0 and (not vert or FFN_TILE_STEPS == GRID_W)
    return pl.pallas_call(
        functools.partial(_ffn_up_conv_kernel, vert=vert, nt=nt),
        grid=(nb, nt + 1),
        in_specs=[pl.BlockSpec((tr, d), lambda p, i: (jnp.minimum(i, nt - 1), 0)), _const_spec(mod.shape),
                  _const_spec(g_pre.shape),
                  pl.BlockSpec((d, fb), lambda p, i: (0, p)), pl.BlockSpec((d, fb), lambda p, i: (0, nb + p)),
                  pl.BlockSpec((dwf.shape[0], fb), lambda p, i: (0, p)),
                  pl.BlockSpec((dwf.shape[0], fb), lambda p, i: (0, nb + p))],
        out_specs=pl.BlockSpec((tr, fb), lambda p, i: (jnp.maximum(i - 1, 0), p)),
        out_shape=jax.ShapeDtypeStruct((rows, f), BF16),
        scratch_shapes=[pltpu.VMEM((3, 2 * fb // FFN_COLS, tr, FFN_COLS), F32)],
        compiler_params=_params(("arbitrary", "arbitrary")),
        name="ffn_up_conv",
    )(x, mod, g_pre, w_up, w_up, dwf, dwf)


def _ffn_down_kernel(hh_ref, wd_ref, x_ref, mod_ref, gpost_ref, o_ref):
    d = x_ref.shape[1]
    y = jnp.dot(hh_ref[...], wd_ref[...], preferred_element_type=F32)
    o_ref[...] = x_ref[...] + _per_batch(_rms(y, gpost_ref[...]), mod_ref[:, 5 * d:6 * d], None)


def _ffn_down(hh, w_down, x, mod, gpost, tr):
    rows, d = x.shape
    f = hh.shape[1]
    row = lambda i: (i, 0)
    return pl.pallas_call(
        _ffn_down_kernel,
        grid=(rows // tr,),
        in_specs=[pl.BlockSpec((tr, f), row), _const_spec(w_down.shape), pl.BlockSpec((tr, d), row),
                  _const_spec(mod.shape), _const_spec(gpost.shape)],
        out_specs=pl.BlockSpec((tr, d), row),
        out_shape=jax.ShapeDtypeStruct((rows, d), F32),
        compiler_params=_params(("arbitrary",)),
        name="ffn_down",
    )(hh, w_down, x, mod, gpost)


def _gate_blocks(w_r, w_i):
    h, hd, _ = w_r.shape
    hpb = GATE_BLK // hd
    nblk = h // hpb
    eye = jnp.eye(hpb, dtype=w_r.dtype)

    def bd(w):
        w = w.reshape(nblk, hpb, hd, hd)
        return (w[:, :, :, None, :] * eye[None, :, None, :, None]).reshape(nblk, GATE_BLK, GATE_BLK)

    return (0.5 * jnp.concatenate([bd(w_r), bd(w_i)], axis=-1)).astype(BF16)


def _row_tile(rows, want):
    t = min(want, rows)
    while rows % t:
        t //= 2
    return t


def kernel(x, c, ctx, c_ctx, w_ada, b_ada, g_pre_mix, g_post_mix, g_pre_ffn, g_post_ffn, w_in, dw_conv,
           ln_conv_g, ln_conv_b, w_proj_conv, lru_conv_w, lru_conv_b, w_rgate, b_rgate, w_igate, b_igate,
           lru_lambda, w_proj_lru, w_out, w_up, dw_ffn, w_down):
    b, t_lat, d = x.shape
    t_ctx = ctx.shape[1]
    depth = w_in.shape[0]
    cw = dw_conv.shape[-1]
    lw = lru_conv_w.shape[-1]
    f = w_down.shape[1]
    assert b == NB and t_lat % GRID_W == 0

    xl = jnp.transpose(x, (1, 0, 2)).reshape(t_lat * NB, d)
    xc_ = jnp.transpose(ctx, (1, 0, 2)).reshape(t_ctx * NB, d)

    cc = jnp.zeros((2 * NB, d), F32).at[0:NB].set(c).at[NB].set(c_ctx)
    mod_all = _ada_call(cc, w_ada, b_ada)

    row2 = lambda a: a.reshape(1, -1)
    zeros_state = jnp.zeros((NB, lw), F32)

    for l in range(depth):
        need_ctx = l < depth - 1
        mod_lat = mod_all[l, 0:NB]
        mod_ctx = jnp.broadcast_to(mod_all[l, NB:NB + 1], (NB, mod_all.shape[-1]))
        w_in_l = w_in[l].astype(BF16)
        wg = [_gate_blocks(w_rgate[l, dd], w_igate[l, dd]) for dd in range(2)]
        br = [row2(b_rgate[l, dd]) for dd in range(2)]
        bi = [row2(b_igate[l, dd]) for dd in range(2)]
        lam = [row2(lru_lambda[l, dd]) for dd in range(2)]
        wpc = w_proj_conv[l].astype(BF16)
        wpl = w_proj_lru[l].astype(BF16)
        wo = w_out[l].astype(BF16)
        wup = w_up[l].astype(BF16)
        wdn = w_down[l].astype(BF16)
        dwf = dw_ffn[l].reshape(-1, 2 * f)

        def mixer(xrows, mod, h0_b, h0_f, state_only=False):
            rows = xrows.shape[0]
            tr = _row_tile(rows, MIX_TILE_ROWS)
            front = _mixer_front(
                xrows, mod, row2(g_pre_mix[l]), w_in_l, lru_conv_w[l], row2(lru_conv_b[l]),
                (wg[0], br[0], bi[0], lam[0]), (wg[1], br[1], bi[1], lam[1]), h0_b, cw, lw, tr, state_only)
            if state_only:
                a_f, u_f, _, st_b = front
                return None, st_b, _scan_state(a_f, u_f, h0_f, tr)
            act, ggt, smg, a_f, u_f, hbwd, st_b = front
            x1, st_f = _mixer_back(act, a_f, u_f, hbwd, ggt, smg, xrows, mod, dw_conv[l], row2(ln_conv_g[l]),
                                   row2(ln_conv_b[l]), wpc, wpl, wo, row2(g_post_mix[l]), h0_f, tr)
            return x1, st_b, st_f

        def ffn(x1, mod, vert):
            hh = _ffn_up_conv(x1, mod, row2(g_pre_ffn[l]), wup, dwf, vert)
            return _ffn_down(hh, wdn, x1, mod, row2(g_post_ffn[l]), _row_tile(x1.shape[0], MIX_TILE_ROWS))

        c1, st_b, st_f = mixer(xc_, mod_ctx, zeros_state, zeros_state, state_only=not need_ctx)
        x1, _, _ = mixer(xl, mod_lat, st_b, st_f)
        xl = ffn(x1, mod_lat, True)
        if need_ctx:
            xc_ = ffn(c1, mod_ctx, False)

    return jnp.transpose(xl.reshape(t_lat, NB, d), (1, 0, 2))
```

```python
import functools

import jax
import jax.numpy as jnp
from jax import lax
from jax.experimental import pallas as pl
from jax.experimental.pallas import tpu as pltpu

F32 = jnp.float32
BF16 = jnp.bfloat16

NB = 8
EPS = 1e-6
GRID_W = 64
LRU_C = 8.0
GATE_BLK = 256
V7X_VMEM_LIMIT = 56 * 1024 * 1024
CONV_HALO_ROWS = 128
LRU_HALO_ROWS = 16
FFN_TILE_STEPS = GRID_W
FFN_COLS = 256
FFN_DOT_ROWS = 128
FFN_CHANNEL_BLOCKS = 2
TM_CHUNK = 8
MIX_TILE_ROWS = 512
MIX_PARTS = 2
MIX_COLS = 256
MIX_DOT_ROWS = 256
LRU_TASK_ROWS = 256


def _const_spec(shape):
    nd = len(shape)
    return pl.BlockSpec(shape, lambda *_: (0,) * nd, pipeline_mode=pl.Buffered(1))


def _params(sem):
    return pltpu.CompilerParams(dimension_semantics=sem, vmem_limit_bytes=V7X_VMEM_LIMIT)


def _rms(x, g):
    ms = jnp.mean(x * x, axis=-1, keepdims=True)
    return x * lax.rsqrt(ms + EPS) * g


def _per_batch(x, mul, add):
    r, d = x.shape
    y = x.reshape(r // NB, NB, d) * mul[None]
    if add is not None:
        y = y + add[None]
    return y.reshape(r, d)


def _sigmoid(x):
    return 0.5 * jnp.tanh(0.5 * x) + 0.5


def _silu(x):
    return x * _sigmoid(x)


def _softplus(z):
    return jnp.maximum(z, 0.0) + jnp.log1p(jnp.exp(-jnp.abs(z)))


def _interleave(stages):
    for mxu_tasks, vpu_tasks in stages:
        nm, nv = len(mxu_tasks), len(vpu_tasks)
        done = 0
        for k, task in enumerate(mxu_tasks):
            task()
            want = -(-(k + 1) * nv // nm)
            for v in vpu_tasks[done:want]:
                v()
            done = max(done, want)
        for v in vpu_tasks[done:]:
            v()


def _ada_kernel(c_ref, w_ref, b_ref, o_ref):
    s = jax.nn.silu(c_ref[...])
    o_ref[0] = jnp.dot(s, w_ref[0], preferred_element_type=F32) + b_ref[0]


def _ada_call(cc, w_ada, b_ada):
    depth, d, n = w_ada.shape
    tn = 1024 if n % 1024 == 0 else n
    return pl.pallas_call(
        _ada_kernel,
        grid=(depth, n // tn),
        in_specs=[pl.BlockSpec(cc.shape, lambda l, j: (0, 0)),
                  pl.BlockSpec((1, d, tn), lambda l, j: (l, 0, j)),
                  pl.BlockSpec((1, 1, tn), lambda l, j: (l, 0, j))],
        out_specs=pl.BlockSpec((1, cc.shape[0], tn), lambda l, j: (l, 0, j)),
        out_shape=jax.ShapeDtypeStruct((depth, cc.shape[0], n), F32),
        compiler_params=_params(("arbitrary", "arbitrary")),
        name="ada_mod",
    )(cc, w_ada, b_ada.reshape(depth, 1, n))


def _mixer_front_kernel(xp_ref, x_ref, xn_ref, mod_ref, g_ref, w_ref, cw_ref, cb_ref,
                        wgf_ref, brf_ref, bif_ref, lamf_ref, wgb_ref, brb_ref, bib_ref, lamb_ref, h0_ref,
                        *refs, cw, lw, state_only):
    if state_only:
        af_ref, uf_ref, hb_ref, st_ref, buf, xc_s, ab_s, ub_s, h_s = refs
    else:
        act_ref, ggt_ref, smg_ref, af_ref, uf_ref, hb_ref, st_ref, buf, xc_s, ab_s, ub_s, h_s = refs
    step_id = pl.program_id(0)
    n = pl.num_programs(0)
    i = n - 1 - step_id
    r, d = x_ref.shape

    @pl.when(step_id == 0)
    def _():
        h_s[...] = h0_ref[...]

    k_taps = cw_ref.shape[0]
    o_xr, o_gt, o_mg = 2 * cw, 2 * cw + lw, 2 * cw + 2 * lw

    def normed(x):
        h = _rms(x, g_ref[...])
        return _per_batch(h, 1.0 + mod_ref[:, d:2 * d], mod_ref[:, 0:d]).astype(BF16)

    hb = normed(x_ref[...])
    hb_halo = normed(jnp.concatenate([xp_ref[...], xn_ref[...]], axis=0))

    def mm(r0, c0):
        return jnp.dot(hb[r0:r0 + MIX_DOT_ROWS], w_ref[:, c0:c0 + MIX_COLS], preferred_element_type=F32)

    def xr_task(c0, r0):
        buf[NB + r0:NB + r0 + MIX_DOT_ROWS, c0:c0 + MIX_COLS] = mm(r0, o_xr + c0)

    def xr_halo_task(c0):
        xh = jnp.dot(hb_halo, w_ref[:, o_xr + c0:o_xr + c0 + MIX_COLS], preferred_element_type=F32)
        buf[0:NB, c0:c0 + MIX_COLS] = jnp.where(i > 0, xh[LRU_HALO_ROWS - NB:LRU_HALO_ROWS], 0.0)
        buf[NB + r:, c0:c0 + MIX_COLS] = jnp.where(i < n - 1, xh[LRU_HALO_ROWS:], 0.0)

    def conv_task(c0):
        xc = cb_ref[:, c0:c0 + MIX_COLS] + cw_ref[0:1, c0:c0 + MIX_COLS] * buf[0:r, c0:c0 + MIX_COLS]
        for k in range(1, k_taps):
            xc = xc + cw_ref[k:k + 1, c0:c0 + MIX_COLS] * buf[k * NB:k * NB + r, c0:c0 + MIX_COLS]
        xc_s[:, c0:c0 + MIX_COLS] = xc

    def act_task(c0, r0):
        act_ref[r0:r0 + MIX_DOT_ROWS, c0:c0 + MIX_COLS] = (
            mm(r0, c0) * _sigmoid(mm(r0, cw + c0))).astype(act_ref.dtype)

    def out_task(ref, fn, c_in, c_out, r0):
        ref[r0:r0 + MIX_DOT_ROWS, c_out:c_out + MIX_COLS] = fn(mm(r0, c_in)).astype(ref.dtype)

    row_starts = range(0, r, MIX_DOT_ROWS)
    xr_tasks = ([functools.partial(xr_task, c, r0) for c in range(0, lw, MIX_COLS) for r0 in row_starts]
                + [functools.partial(xr_halo_task, c) for c in range(0, lw, MIX_COLS)])
    coeffs = (_lru_coeff_tasks(xc_s, wgf_ref, brf_ref, bif_ref, lamf_ref, af_ref, uf_ref)
              + _lru_coeff_tasks(xc_s, wgb_ref, brb_ref, bib_ref, lamb_ref, ab_s, ub_s))
    convs = [functools.partial(conv_task, c) for c in range(0, lw, MIX_COLS)]
    if state_only:
        _interleave([(xr_tasks, []), ([], convs + coeffs)])
    else:
        act_tasks = [functools.partial(act_task, c, r0) for c in range(0, cw, MIX_COLS) for r0 in row_starts]
        rest = ([functools.partial(out_task, ggt_ref, jax.nn.gelu, o_gt + c, c, r0)
                 for c in range(0, lw, MIX_COLS) for r0 in row_starts]
                + [functools.partial(out_task, smg_ref, _sigmoid, o_mg + c, c, r0)
                   for c in range(0, 2 * d, MIX_COLS) for r0 in row_starts])
        n_head = len(act_tasks) // 4
        _interleave([(xr_tasks, []), (act_tasks[:n_head], convs), (act_tasks[n_head:] + rest, coeffs)])

    nt = r // NB

    def step(s, h):
        r0 = pl.multiple_of((nt - 1 - s) * NB, NB)
        h = ab_s[pl.ds(r0, NB), :] * h + ub_s[pl.ds(r0, NB), :]
        hb_ref[pl.ds(r0, NB), :] = h
        return h

    h = lax.fori_loop(0, nt, step, h_s[...], unroll=8)
    h_s[...] = h
    st_ref[...] = h


def _mixer_front(x, mod, g_pre, w_in, conv_w, conv_b, gates_f, gates_b, h0_b, cw, lw, tr, state_only=False):
    rows, d = x.shape
    n = rows // tr
    hb = tr // LRU_HALO_ROWS
    last = rows // LRU_HALO_ROWS - 1
    assert tr % MIX_DOT_ROWS == 0 and cw % MIX_COLS == 0 and lw % MIX_COLS == 0 and d % MIX_COLS == 0
    kern = functools.partial(_mixer_front_kernel, cw=cw, lw=lw, state_only=state_only)
    row = lambda s: (n - 1 - s, 0)
    consts = [mod, g_pre, w_in, conv_w, conv_b, *gates_f, *gates_b, h0_b]
    f32_out = jax.ShapeDtypeStruct((rows, lw), F32)
    main_specs = [pl.BlockSpec((tr, cw), row), pl.BlockSpec((tr, lw), row), pl.BlockSpec((tr, 2 * d), row)]
    main_shapes = [jax.ShapeDtypeStruct((rows, cw), BF16), jax.ShapeDtypeStruct((rows, lw), BF16),
                   jax.ShapeDtypeStruct((rows, 2 * d), BF16)]
    return pl.pallas_call(
        kern,
        grid=(n,),
        in_specs=[pl.BlockSpec((LRU_HALO_ROWS, d), lambda s: (jnp.maximum((n - 1 - s) * hb - 1, 0), 0)),
                  pl.BlockSpec((tr, d), row),
                  pl.BlockSpec((LRU_HALO_ROWS, d), lambda s: (jnp.minimum((n - s) * hb, last), 0))]
                 + [_const_spec(a.shape) for a in consts],
        out_specs=([] if state_only else main_specs)
                  + [pl.BlockSpec((tr, lw), row)] * 3 + [pl.BlockSpec((NB, lw), lambda s: (0, 0))],
        out_shape=([] if state_only else main_shapes)
                  + [f32_out, f32_out, f32_out, jax.ShapeDtypeStruct((NB, lw), F32)],
        scratch_shapes=[pltpu.VMEM((tr + NB + LRU_HALO_ROWS, lw), F32), pltpu.VMEM((tr, lw), F32),
                        pltpu.VMEM((tr, lw), F32), pltpu.VMEM((tr, lw), F32), pltpu.VMEM((NB, lw), F32)],
        compiler_params=_params(("arbitrary",)),
        name="mixer_front",
    )(x, x, x, *consts)


def _lru_coeff_tasks(xc, wg_ref, br_ref, bi_ref, lam_ref, a_s, u_s):
    def block(blk, r0):
        lo, hi = blk * GATE_BLK, (blk + 1) * GATE_BLK
        x = xc[r0:r0 + LRU_TASK_ROWS, lo:hi]
        g = jnp.dot(x.astype(BF16), wg_ref[blk], preferred_element_type=F32)
        t_r = jnp.tanh(g[:, :GATE_BLK] + 0.5 * br_ref[:, lo:hi])
        t_i = jnp.tanh(g[:, GATE_BLK:] + 0.5 * bi_ref[:, lo:hi])
        hc = (-0.5 * LRU_C) * _softplus(-lam_ref[:, lo:hi])
        log_a = hc * t_r + hc
        a_s[r0:r0 + LRU_TASK_ROWS, lo:hi] = jnp.exp(log_a)
        th = jnp.tanh(log_a)
        u_s[r0:r0 + LRU_TASK_ROWS, lo:hi] = jnp.sqrt(-0.5 * th) * lax.rsqrt(1.0 - th) * (t_i + 1.0) * x

    return [functools.partial(block, blk, r0) for blk in range(wg_ref.shape[0])
            for r0 in range(0, xc.shape[0], LRU_TASK_ROWS)]


def _scan_state_kernel(a_ref, u_ref, h0_ref, st_ref, h_s):
    nt = a_ref.shape[0] // NB

    @pl.when(pl.program_id(0) == 0)
    def _():
        h_s[...] = h0_ref[...]

    def step(s, h):
        r0 = pl.multiple_of(s * NB, NB)
        return a_ref[pl.ds(r0, NB), :] * h + u_ref[pl.ds(r0, NB), :]

    h = lax.fori_loop(0, nt, step, h_s[...], unroll=8)
    h_s[...] = h
    st_ref[...] = h


def _scan_state(a, u, h0, tr):
    rows, w = a.shape
    row = lambda i: (i, 0)
    return pl.pallas_call(
        _scan_state_kernel,
        grid=(rows // tr,),
        in_specs=[pl.BlockSpec((tr, w), row), pl.BlockSpec((tr, w), row), _const_spec(h0.shape)],
        out_specs=pl.BlockSpec((NB, w), lambda i: (0, 0)),
        out_shape=jax.ShapeDtypeStruct((NB, w), F32),
        scratch_shapes=[pltpu.VMEM((NB, w), F32)],
        compiler_params=_params(("arbitrary",)),
        name="lru_scan_state",
    )(a, u, h0)


def _dwconv_time(src, dst, w_ref, rows, k_taps):
    lanes = dst.shape[1]
    span = TM_CHUNK + k_taps - 1
    for cb in range(lanes // 128):
        lo, hi = cb * 128, (cb + 1) * 128
        wk = [jnp.broadcast_to(w_ref[k:k + 1, lo:hi], (NB, 128)) for k in range(k_taps)]

        def chunk(c, carry, lo=lo, hi=hi, wk=wk):
            base = pl.multiple_of(c * (TM_CHUNK * NB), TM_CHUNK * NB)
            ins = [src[pl.ds(base + q * NB, NB), lo:hi] for q in range(span)]
            outs = []
            for m in range(TM_CHUNK):
                acc = wk[0] * ins[m]
                for k in range(1, k_taps):
                    acc = acc + wk[k] * ins[m + k]
                outs.append(acc)
            dst[pl.ds(base, TM_CHUNK * NB), lo:hi] = jnp.concatenate(outs, axis=0)
            return carry

        lax.fori_loop(0, rows // (TM_CHUNK * NB), chunk, 0, unroll=4)


def _mixer_back_kernel(ap_ref, am_ref, an_ref, af_ref, uf_ref, hb_ref, ggt_ref, smg_ref, x_ref, mod_ref,
                       dw_ref, lng_ref, lnb_ref, wpc_ref, wpl_ref, wo_ref, gpost_ref, h0_ref,
                       x1_ref, st_ref, cbuf, cout, yb_s, y_s, h_s, lhs_s, ya_s, m_s):
    i = pl.program_id(0)
    n = pl.num_programs(0)
    r, d = x_ref.shape
    k_taps = dw_ref.shape[0]
    halo = (k_taps // 2) * NB

    @pl.when(i == 0)
    def _():
        h_s[...] = h0_ref[...]

    prev = ap_ref[...].astype(F32)
    nxt = an_ref[...].astype(F32)
    cbuf[0:halo, :] = jnp.where(i > 0, prev[CONV_HALO_ROWS - halo:, :], 0.0)
    cbuf[halo:halo + r, :] = am_ref[...].astype(F32)
    cbuf[halo + r:, :] = jnp.where(i < n - 1, nxt[0:halo, :], 0.0)
    _dwconv_time(cbuf, cout, dw_ref, r, k_taps)

    parts = [(p * (r // MIX_PARTS), r // MIX_PARTS) for p in range(MIX_PARTS)]
    col_blocks = range(0, d, MIX_COLS)

    def matmul_tasks(lhs, w_ref, out, r0, rn):
        def piece(c0):
            out[r0:r0 + rn, c0:c0 + MIX_COLS] = jnp.dot(lhs[r0:r0 + rn, :], w_ref[:, c0:c0 + MIX_COLS],
                                                        preferred_element_type=F32)
        return [functools.partial(piece, c0) for c0 in col_blocks]

    def ln_task(r0, rn):
        u = cout[r0:r0 + rn, :]
        mu = jnp.mean(u, axis=-1, keepdims=True)
        uc = u - mu
        var = jnp.mean(uc * uc, axis=-1, keepdims=True)
        ln = uc * lax.rsqrt(var + EPS) * lng_ref[...] + lnb_ref[...]
        lhs_s[r0:r0 + rn, :] = _silu(ln).astype(BF16)

    stages = [([], [functools.partial(ln_task, *parts[0])])]
    for p, part in enumerate(parts):
        vpu = [functools.partial(ln_task, *parts[p + 1])] if p + 1 < len(parts) else []
        stages.append((matmul_tasks(lhs_s, wpc_ref, ya_s, *part), vpu))
    _interleave(stages)

    nt = r // NB

    def step(s, h):
        r0 = pl.multiple_of(s * NB, NB)
        h = af_ref[pl.ds(r0, NB), :] * h + uf_ref[pl.ds(r0, NB), :]
        cout[pl.ds(r0, NB), :] = h + hb_ref[pl.ds(r0, NB), :]
        return h

    h = lax.fori_loop(0, nt, step, h_s[...], unroll=8)
    h_s[...] = h
    st_ref[...] = h

    def gate_task(r0, rn):
        lhs_s[r0:r0 + rn, :] = (cout[r0:r0 + rn, :] * ggt_ref[r0:r0 + rn, :].astype(F32)).astype(BF16)

    def merge_task(r0, rn):
        m = (smg_ref[r0:r0 + rn, 0:d].astype(F32) * ya_s[r0:r0 + rn, :]
             + smg_ref[r0:r0 + rn, d:2 * d].astype(F32) * yb_s[r0:r0 + rn, :])
        m_s[r0:r0 + rn, :] = m.astype(BF16)

    def out_task(r0, rn):
        y = _rms(y_s[r0:r0 + rn, :], gpost_ref[...])
        x1_ref[r0:r0 + rn, :] = x_ref[r0:r0 + rn, :] + _per_batch(y, mod_ref[:, 2 * d:3 * d], None)

    np_ = len(parts)
    stages = []
    for s in range(np_ + 4):
        mxu, vpu = [], []
        if s < np_:
            vpu.append(functools.partial(gate_task, *parts[s]))
        if 1 <= s <= np_:
            mxu += matmul_tasks(lhs_s, wpl_ref, yb_s, *parts[s - 1])
        if 2 <= s <= np_ + 1:
            vpu.append(functools.partial(merge_task, *parts[s - 2]))
        if 3 <= s <= np_ + 2:
            mxu += matmul_tasks(m_s, wo_ref, y_s, *parts[s - 3])
        if 4 <= s <= np_ + 3:
            vpu.append(functools.partial(out_task, *parts[s - 4]))
        stages.append((mxu, vpu))
    _interleave(stages)


def _mixer_back(act, a_f, u_f, hb, ggt, smg, x, mod, dw, lng, lnb, wpc, wpl, wo, gpost, h0, tr):
    rows, d = x.shape
    cw = act.shape[1]
    w = a_f.shape[1]
    assert cw == w == d, "conv / LRU / model widths share scratch buffers"
    assert tr % (MIX_PARTS * 16) == 0 and d % MIX_COLS == 0
    n = rows // tr
    hb_blocks = tr // CONV_HALO_ROWS
    last = rows // CONV_HALO_ROWS - 1
    k_taps = dw.shape[0]
    halo = (k_taps // 2) * NB
    assert halo <= CONV_HALO_ROWS
    row = lambda i: (i, 0)
    consts = [mod, dw, lng, lnb, wpc, wpl, wo, gpost, h0]
    return pl.pallas_call(
        _mixer_back_kernel,
        grid=(n,),
        in_specs=[pl.BlockSpec((CONV_HALO_ROWS, cw), lambda i: (jnp.maximum(i * hb_blocks - 1, 0), 0)),
                  pl.BlockSpec((tr, cw), row),
                  pl.BlockSpec((CONV_HALO_ROWS, cw), lambda i: (jnp.minimum((i + 1) * hb_blocks, last), 0)),
                  pl.BlockSpec((tr, w), row), pl.BlockSpec((tr, w), row), pl.BlockSpec((tr, w), row),
                  pl.BlockSpec((tr, w), row), pl.BlockSpec((tr, 2 * d), row), pl.BlockSpec((tr, d), row)]
                 + [_const_spec(a.shape) for a in consts],
        out_specs=[pl.BlockSpec((tr, d), row), pl.BlockSpec((NB, w), lambda i: (0, 0))],
        out_shape=[jax.ShapeDtypeStruct((rows, d), F32), jax.ShapeDtypeStruct((NB, w), F32)],
        scratch_shapes=[pltpu.VMEM((tr + 2 * halo, cw), F32), pltpu.VMEM((tr, cw), F32),
                        pltpu.VMEM((tr, w), F32), pltpu.VMEM((tr, w), F32), pltpu.VMEM((NB, w), F32),
                        pltpu.VMEM((tr, w), BF16), pltpu.VMEM((tr, d), F32), pltpu.VMEM((tr, d), BF16)],
        compiler_params=_params(("arbitrary",)),
        name="mixer_back",
    )(act, act, act, a_f, u_f, hb, ggt, smg, x, *consts)


def _ffn_up_conv_kernel(x_ref, mod_ref, g_ref, wv_ref, wg_ref, dwv_ref, dwg_ref, hh_ref, zr, *, vert, nt):
    i = pl.program_id(1)
    r, d = x_ref.shape
    f = hh_ref.shape[1]
    k = 3
    nchunk = r // (TM_CHUNK * NB)
    slot_c = 0
    slot_b = 2
    slot_a = 1
    has_before = i > 1
    has_after = i < nt

    @pl.when(i == 0)
    def _():
        zr[1] = jnp.zeros(zr.shape[1:], F32)
        zr[2] = jnp.zeros(zr.shape[1:], F32)

    def conv_tasks(lo):
        arrays = []
        for dw_ref, col in ((dwv_ref, lo), (dwg_ref, f + lo)):
            w = {(dy, dx): jnp.broadcast_to(dw_ref[dy * k + dx:dy * k + dx + 1, lo:lo + 128], (NB, 128))
                 for dy in range(k) for dx in range(k)}
            if vert:
                srcs = ((slot_a, {dx: jnp.where(has_before, w[(0, dx)], 0.0) for dx in range(k)}),
                        (slot_b, {dx: w[(1, dx)] for dx in range(k)}),
                        (slot_c, {dx: jnp.where(has_after, w[(2, dx)], 0.0) for dx in range(k)}))
                edge = None
            else:
                srcs = ((slot_b, {dx: w[(1, dx)] for dx in range(k)}),)
                edge = (jnp.where(has_before, w[(1, 0)], 0.0), jnp.where(has_after, w[(1, 2)], 0.0))
            arrays.append((col, srcs, edge))
        return [functools.partial(conv_chunk, lo, arrays, q) for q in range(nchunk)]

    def conv_chunk(lo, arrays, q):
        base = q * TM_CHUNK * NB
        lo_r = max(base - NB, 0)
        hi_r = min(base + (TM_CHUNK + 1) * NB, r)
        res = []
        for col, srcs, edge in arrays:
            pc, ln = col // FFN_COLS, col % FFN_COLS
            blks = [zr[slot, pc, lo_r:hi_r, ln:ln + 128] for slot, _ in srcs]
            outs = []
            for m in range(TM_CHUNK):
                acc = None
                for (_, wt), blk in zip(srcs, blks):
                    for dx in range(k):
                        row = base + (m + dx - 1) * NB
                        if row < 0 or row >= r:
                            continue
                        term = wt[dx] * blk[row - lo_r:row - lo_r + NB]
                        acc = term if acc is None else acc + term
                if edge is not None and q == 0 and m == 0:
                    acc = acc + edge[0] * zr[slot_a, pc, r - NB:r, ln:ln + 128]
                if edge is not None and q == nchunk - 1 and m == TM_CHUNK - 1:
                    acc = acc + edge[1] * zr[slot_c, pc, 0:NB, ln:ln + 128]
                outs.append(acc)
            res.append(jnp.concatenate(outs, axis=0))
        half_g = 0.5 * res[1]
        hh_ref[base:base + TM_CHUNK * NB, lo:lo + 128] = (
            (half_g * res[0]) * (jnp.tanh(half_g) + 1.0)).astype(hh_ref.dtype)

    h = _rms(x_ref[...], g_ref[...])
    h = _per_batch(h, 1.0 + mod_ref[:, 4 * d:5 * d], mod_ref[:, 3 * d:4 * d])
    hb = h.astype(BF16)
    nk = f // FFN_COLS

    def dot_task(w_ref, off, c0, r0):
        zr[slot_c, (off + c0) // FFN_COLS, r0:r0 + FFN_DOT_ROWS] = jnp.dot(
            hb[r0:r0 + FFN_DOT_ROWS], w_ref[:, c0:c0 + FFN_COLS], preferred_element_type=F32)

    def age_task(pc):
        zr[slot_a, pc] = zr[slot_b, pc]
        zr[slot_b, pc] = zr[slot_c, pc]

    stages = []
    for kk in range(nk + 1):
        mxu_tasks, vpu_tasks = [], []
        if kk < nk:
            c0 = kk * FFN_COLS
            mxu_tasks = [functools.partial(dot_task, w_ref, off, c0, r0)
                         for w_ref, off in ((wv_ref, 0), (wg_ref, f)) for r0 in range(0, r, FFN_DOT_ROWS)]
        if kk >= 1:
            c0 = (kk - 1) * FFN_COLS
            for cb in range(FFN_COLS // 128):
                vpu_tasks += conv_tasks(c0 + cb * 128)
            vpu_tasks += [functools.partial(age_task, (off + c0) // FFN_COLS) for off in (0, f)]
        stages.append((mxu_tasks, vpu_tasks))
    _interleave(stages)


def _ffn_up_conv(x, mod, g_pre, w_up, dwf, vert):
    rows, d = x.shape
    f = w_up.shape[1] // 2
    tr = FFN_TILE_STEPS * NB
    nt = rows // tr
    fb = f // FFN_CHANNEL_BLOCKS
    nb = FFN_CHANNEL_BLOCKS
    assert rows % tr == 0 and fb % FFN_COLS == 0 and (not vert or FFN_TILE_STEPS == GRID_W)
    return pl.pallas_call(
        functools.partial(_ffn_up_conv_kernel, vert=vert, nt=nt),
        grid=(nb, nt + 1),
        in_specs=[pl.BlockSpec((tr, d), lambda p, i: (jnp.minimum(i, nt - 1), 0)), _const_spec(mod.shape),
                  _const_spec(g_pre.shape),
                  pl.BlockSpec((d, fb), lambda p, i: (0, p)), pl.BlockSpec((d, fb), lambda p, i: (0, nb + p)),
                  pl.BlockSpec((dwf.shape[0], fb), lambda p, i: (0, p)),
                  pl.BlockSpec((dwf.shape[0], fb), lambda p, i: (0, nb + p))],
        out_specs=pl.BlockSpec((tr, fb), lambda p, i: (jnp.maximum(i - 1, 0), p)),
        out_shape=jax.ShapeDtypeStruct((rows, f), BF16),
        scratch_shapes=[pltpu.VMEM((3, 2 * fb // FFN_COLS, tr, FFN_COLS), F32)],
        compiler_params=_params(("arbitrary", "arbitrary")),
        name="ffn_up_conv",
    )(x, mod, g_pre, w_up, w_up, dwf, dwf)


def _ffn_down_kernel(hh_ref, wd_ref, x_ref, mod_ref, gpost_ref, o_ref, *scratch, batch_major):
    r, d = x_ref.shape
    y = jnp.dot(hh_ref[...], wd_ref[...], preferred_element_type=F32)
    res = x_ref[...] + _per_batch(_rms(y, gpost_ref[...]), mod_ref[:, 5 * d:6 * d], None)
    if batch_major:
        res_s, = scratch
        for cb in range(d // 128):
            res_s[cb] = res[:, cb * 128:(cb + 1) * 128]
            for b in range(NB):
                o_ref[b, :, cb * 128:(cb + 1) * 128] = res_s[cb, pl.ds(b, r // NB, stride=NB), :]
    else:
        o_ref[...] = res


def _ffn_down(hh, w_down, x, mod, gpost, tr, batch_major=False):
    rows, d = x.shape
    f = hh.shape[1]
    row = lambda i: (i, 0)
    if batch_major:
        out_spec = pl.BlockSpec((NB, tr // NB, d), lambda i: (0, i, 0))
        out_shape = jax.ShapeDtypeStruct((NB, rows // NB, d), F32)
        scratch = [pltpu.VMEM((d // 128, tr, 128), F32)]
    else:
        out_spec, out_shape, scratch = pl.BlockSpec((tr, d), row), jax.ShapeDtypeStruct((rows, d), F32), []
    return pl.pallas_call(
        functools.partial(_ffn_down_kernel, batch_major=batch_major),
        grid=(rows // tr,),
        in_specs=[pl.BlockSpec((tr, f), row), _const_spec(w_down.shape), pl.BlockSpec((tr, d), row),
                  _const_spec(mod.shape), _const_spec(gpost.shape)],
        out_specs=out_spec,
        out_shape=out_shape,
        scratch_shapes=scratch,
        compiler_params=_params(("arbitrary",)),
        name="ffn_down",
    )(hh, w_down, x, mod, gpost)


def _gate_blocks(w_r, w_i):
    h, hd, _ = w_r.shape
    hpb = GATE_BLK // hd
    nblk = h // hpb
    eye = jnp.eye(hpb, dtype=w_r.dtype)

    def bd(w):
        w = w.reshape(nblk, hpb, hd, hd)
        return (w[:, :, :, None, :] * eye[None, :, None, :, None]).reshape(nblk, GATE_BLK, GATE_BLK)

    return (0.5 * jnp.concatenate([bd(w_r), bd(w_i)], axis=-1)).astype(BF16)


def _row_tile(rows, want):
    t = min(want, rows)
    while rows % t:
        t //= 2
    return t


def kernel(x, c, ctx, c_ctx, w_ada, b_ada, g_pre_mix, g_post_mix, g_pre_ffn, g_post_ffn, w_in, dw_conv,
           ln_conv_g, ln_conv_b, w_proj_conv, lru_conv_w, lru_conv_b, w_rgate, b_rgate, w_igate, b_igate,
           lru_lambda, w_proj_lru, w_out, w_up, dw_ffn, w_down):
    b, t_lat, d = x.shape
    t_ctx = ctx.shape[1]
    depth = w_in.shape[0]
    cw = dw_conv.shape[-1]
    lw = lru_conv_w.shape[-1]
    f = w_down.shape[1]
    assert b == NB and t_lat % GRID_W == 0

    xl = jnp.transpose(x, (1, 0, 2)).reshape(t_lat * NB, d)
    xc_ = jnp.transpose(ctx, (1, 0, 2)).reshape(t_ctx * NB, d)

    cc = jnp.zeros((2 * NB, d), F32).at[0:NB].set(c).at[NB].set(c_ctx)
    mod_all = _ada_call(cc, w_ada, b_ada)

    row2 = lambda a: a.reshape(1, -1)
    zeros_state = jnp.zeros((NB, lw), F32)

    for l in range(depth):
        need_ctx = l < depth - 1
        mod_lat = mod_all[l, 0:NB]
        mod_ctx = jnp.broadcast_to(mod_all[l, NB:NB + 1], (NB, mod_all.shape[-1]))
        w_in_l = w_in[l].astype(BF16)
        wg = [_gate_blocks(w_rgate[l, dd], w_igate[l, dd]) for dd in range(2)]
        br = [row2(b_rgate[l, dd]) for dd in range(2)]
        bi = [row2(b_igate[l, dd]) for dd in range(2)]
        lam = [row2(lru_lambda[l, dd]) for dd in range(2)]
        wpc = w_proj_conv[l].astype(BF16)
        wpl = w_proj_lru[l].astype(BF16)
        wo = w_out[l].astype(BF16)
        wup = w_up[l].astype(BF16)
        wdn = w_down[l].astype(BF16)
        dwf = dw_ffn[l].reshape(-1, 2 * f)

        def mixer(xrows, mod, h0_b, h0_f, state_only=False):
            rows = xrows.shape[0]
            tr = _row_tile(rows, MIX_TILE_ROWS)
            front = _mixer_front(
                xrows, mod, row2(g_pre_mix[l]), w_in_l, lru_conv_w[l], row2(lru_conv_b[l]),
                (wg[0], br[0], bi[0], lam[0]), (wg[1], br[1], bi[1], lam[1]), h0_b, cw, lw, tr, state_only)
            if state_only:
                a_f, u_f, _, st_b = front
                return None, st_b, _scan_state(a_f, u_f, h0_f, tr)
            act, ggt, smg, a_f, u_f, hbwd, st_b = front
            x1, st_f = _mixer_back(act, a_f, u_f, hbwd, ggt, smg, xrows, mod, dw_conv[l], row2(ln_conv_g[l]),
                                   row2(ln_conv_b[l]), wpc, wpl, wo, row2(g_post_mix[l]), h0_f, tr)
            return x1, st_b, st_f

        def ffn(x1, mod, vert, batch_major=False):
            hh = _ffn_up_conv(x1, mod, row2(g_pre_ffn[l]), wup, dwf, vert)
            return _ffn_down(hh, wdn, x1, mod, row2(g_post_ffn[l]), _row_tile(x1.shape[0], MIX_TILE_ROWS),
                             batch_major)

        c1, st_b, st_f = mixer(xc_, mod_ctx, zeros_state, zeros_state, state_only=not need_ctx)
        x1, _, _ = mixer(xl, mod_lat, st_b, st_f)
        xl = ffn(x1, mod_lat, True, batch_major=not need_ctx)
        if need_ctx:
            xc_ = ffn(c1, mod_ctx, False)

    return xl
```
